```python
import math
import jax
import jax.numpy as jnp
from jax import lax
import numpy as np

D_MODEL = 2048
BATCH = 4
SEQ = 4096
DEPTH = 4

N_EVEN = (DEPTH + 1) // 2
N_ODD = DEPTH // 2
Q_BLOCK = 128

HEAD_DIM = 128
SB_HEADS = 8
FOX_HEADS = 8
SB_WIDTH = SB_HEADS * HEAD_DIM
FOX_WIDTH = FOX_HEADS * HEAD_DIM
EVEN_MIX = SB_WIDTH + FOX_WIDTH
EVEN_IN = 3 * SB_WIDTH + 3 * FOX_WIDTH + FOX_HEADS
EVEN_SPLITS = (SB_WIDTH, 2 * SB_WIDTH, 3 * SB_WIDTH,
               3 * SB_WIDTH + FOX_WIDTH, 3 * SB_WIDTH + 2 * FOX_WIDTH,
               3 * SB_WIDTH + 3 * FOX_WIDTH)

MLA_HEADS = 16
Q_LORA = 512
KV_LORA = 512
NOPE_DIM = 128
ROPE_DIM = 64
V_DIM = 128
ROPE_THETA = 10000.0
MLA_DOWN = Q_LORA + KV_LORA + ROPE_DIM

N_EXPERTS = 32
TOP_K = 4
D_EXPERT = 768
SWIGLU_LIMIT = 7.0
SWIGLU_ALPHA = 1.702
EXPERT_BLOCK = 128

DEEPNORM_ALPHA = (2 * DEPTH) ** 0.25
DEEPNORM_BETA = (8 * DEPTH) ** -0.25
LN_EPS = 1e-5
RMS_EPS = 1e-6

kernel_name = "hybrid_stickbreak_fox_mla_moe_deepnorm"


def _layer_norm(x, g, b):
    xf = x.astype(jnp.float32)
    mu = jnp.mean(xf, axis=-1, keepdims=True)
    var = jnp.mean(jnp.square(xf - mu), axis=-1, keepdims=True)
    y = (xf - mu) * lax.rsqrt(var + LN_EPS) * g.astype(jnp.float32) + b.astype(jnp.float32)
    return y.astype(x.dtype)


def _rms_norm(x, g):
    xf = x.astype(jnp.float32)
    y = xf * lax.rsqrt(jnp.mean(xf * xf, axis=-1, keepdims=True) + RMS_EPS) * g.astype(jnp.float32)
    return y.astype(x.dtype)


def _query_blocks(t):
    b, s = t.shape[0], t.shape[1]
    t = t.reshape((b, s // Q_BLOCK, Q_BLOCK) + t.shape[2:])
    return jnp.moveaxis(t, 1, 0)


def _merge_blocks(o):
    o = jnp.moveaxis(o, 0, 1)
    return o.reshape((o.shape[0], o.shape[1] * o.shape[2]) + o.shape[3:])


def _stick_breaking_attention(q, k, v):
    seq = q.shape[1]
    scale = HEAD_DIM ** -0.5
    kpos = jnp.arange(seq)
    starts = jnp.arange(seq // Q_BLOCK) * Q_BLOCK

    def block(args):
        qb, start = args
        qpos = start + jnp.arange(Q_BLOCK)
        z = jnp.einsum('bqhd,bkhd->bhqk', qb, k,
                       preferred_element_type=jnp.float32) * scale
        strict = kpos[None, :] < qpos[:, None]
        log_fail = jnp.where(strict, -jax.nn.softplus(z), 0.0)
        later = lax.cumsum(log_fail, axis=3, reverse=True) - log_fail
        w = jnp.where(strict, jnp.exp(jax.nn.log_sigmoid(z) + later), 0.0)
        return jnp.einsum('bhqk,bkhd->bqhd', w.astype(v.dtype), v)

    return _merge_blocks(lax.map(block, (_query_blocks(q), starts)))


def _forgetting_attention(q, k, v, log_f):
    seq = q.shape[1]
    scale = HEAD_DIM ** -0.5
    kpos = jnp.arange(seq)
    starts = jnp.arange(seq // Q_BLOCK) * Q_BLOCK
    cum = lax.cumsum(log_f, axis=1)
    cum_k = jnp.transpose(cum, (0, 2, 1))

    def block(args):
        qb, cum_q, start = args
        qpos = start + jnp.arange(Q_BLOCK)
        z = jnp.einsum('bqhd,bkhd->bhqk', qb, k,
                       preferred_element_type=jnp.float32) * scale
        decay = jnp.transpose(cum_q, (0, 2, 1))[:, :, :, None] - cum_k[:, :, None, :]
        causal = kpos[None, :] <= qpos[:, None]
        p = jax.nn.softmax(jnp.where(causal, z + decay, -jnp.inf), axis=-1)
        return jnp.einsum('bhqk,bkhd->bqhd', p.astype(v.dtype), v)

    return _merge_blocks(lax.map(block, (_query_blocks(q), _query_blocks(cum), starts)))


def _sb_fox_mixer(x, w_in, b_f, w_o):
    b, s, _ = x.shape
    h = x @ w_in
    q_sb, k_sb, v_sb, q_fx, k_fx, v_fx, f_logit = jnp.split(h, list(EVEN_SPLITS), axis=-1)
    sb = lambda t: t.reshape(b, s, SB_HEADS, HEAD_DIM)
    fx = lambda t: t.reshape(b, s, FOX_HEADS, HEAD_DIM)
    o_sb = _stick_breaking_attention(sb(q_sb), sb(k_sb), sb(v_sb))
    log_f = jax.nn.log_sigmoid(f_logit.astype(jnp.float32) + b_f.astype(jnp.float32))
    o_fx = _forgetting_attention(fx(q_fx), fx(k_fx), fx(v_fx), log_f)
    o = jnp.concatenate([o_sb.reshape(b, s, SB_WIDTH), o_fx.reshape(b, s, FOX_WIDTH)], axis=-1)
    return (o @ w_o).astype(x.dtype)


def _rope(t, cos, sin):
    t1, t2 = jnp.split(t, 2, axis=-1)
    return jnp.concatenate([t1 * cos - t2 * sin, t1 * sin + t2 * cos], axis=-1)


def _mla_mixer(x, positions, w_down, q_norm, kv_norm, w_uq, w_ukv, w_o):
    b, s, _ = x.shape
    down = x @ w_down
    c_q, c_kv, k_pe = jnp.split(down, [Q_LORA, Q_LORA + KV_LORA], axis=-1)
    q = (_rms_norm(c_q, q_norm) @ w_uq).reshape(b, s, MLA_HEADS, NOPE_DIM + ROPE_DIM)
    q_nope, q_pe = jnp.split(q, [NOPE_DIM], axis=-1)
    kv = (_rms_norm(c_kv, kv_norm) @ w_ukv).reshape(b, s, MLA_HEADS, NOPE_DIM + V_DIM)
    k_nope, v = jnp.split(kv, [NOPE_DIM], axis=-1)

    inv_freq = ROPE_THETA ** (-jnp.arange(0, ROPE_DIM, 2, dtype=jnp.float32) / ROPE_DIM)
    ang = positions.astype(jnp.float32)[..., None] * inv_freq
    cos, sin = jnp.cos(ang), jnp.sin(ang)
    q_pe = _rope(q_pe, cos[:, :, None, :], sin[:, :, None, :]).astype(x.dtype)
    k_pe = _rope(k_pe, cos, sin).astype(x.dtype)

    scale = (NOPE_DIM + ROPE_DIM) ** -0.5
    kpos = jnp.arange(s)
    starts = jnp.arange(s // Q_BLOCK) * Q_BLOCK

    def block(args):
        qn, qp, start = args
        qpos = start + jnp.arange(Q_BLOCK)
        z = (jnp.einsum('bqhd,bkhd->bhqk', qn, k_nope, preferred_element_type=jnp.float32)
             + jnp.einsum('bqhr,bkr->bhqk', qp, k_pe, preferred_element_type=jnp.float32)) * scale
        causal = kpos[None, :] <= qpos[:, None]
        p = jax.nn.softmax(jnp.where(causal, z, -jnp.inf), axis=-1)
        return jnp.einsum('bhqk,bkhd->bqhd', p.astype(v.dtype), v)

    o = _merge_blocks(lax.map(block, (_query_blocks(q_nope), _query_blocks(q_pe), starts)))
    return (o.reshape(b, s, MLA_HEADS * V_DIM) @ w_o).astype(x.dtype)


def _moe(x, router_w, router_b, w_gu, b_gu, w_dn, b_dn):
    b, s, d = x.shape
    xt = x.reshape(-1, d)
    n = xt.shape[0]
    logits = jnp.matmul(xt, router_w, preferred_element_type=jnp.float32) + router_b.astype(jnp.float32)
    top_logit, top_e = lax.top_k(logits, TOP_K)
    gate = jax.nn.softmax(top_logit, axis=-1)

    n_assign = n * TOP_K
    flat_e = top_e.reshape(-1)
    order = jnp.argsort(flat_e)
    sorted_e = flat_e[order]
    sorted_tok = (order // TOP_K).astype(jnp.int32)
    sorted_gate = gate.reshape(-1)[order]
    counts = jnp.bincount(flat_e, length=N_EXPERTS)
    padded = (counts + EXPERT_BLOCK - 1) // EXPERT_BLOCK * EXPERT_BLOCK
    start = jnp.cumsum(counts) - counts
    pend = jnp.cumsum(padded)
    pstart = pend - padded
    dest = pstart[sorted_e] + jnp.arange(n_assign) - start[sorted_e]
    n_rows = -(-(n_assign + N_EXPERTS * (EXPERT_BLOCK - 1)) // EXPERT_BLOCK) * EXPERT_BLOCK
    n_blk = n_rows // EXPERT_BLOCK
    row_tok = jnp.full((n_rows,), n, jnp.int32).at[dest].set(sorted_tok)
    row_gate = jnp.zeros((n_rows,), jnp.float32).at[dest].set(sorted_gate)
    blk_e = jnp.minimum(jnp.searchsorted(pend, jnp.arange(n_blk) * EXPERT_BLOCK, side='right'),
                        N_EXPERTS - 1)
    x_pad = jnp.concatenate([xt, jnp.zeros((1, d), xt.dtype)], axis=0)

    def expert_block(args):
        tok, e = args
        hb = x_pad[tok] @ w_gu[e] + b_gu[e]
        g, u = jnp.split(hb, 2, axis=-1)
        g = jnp.minimum(g, SWIGLU_LIMIT)
        u = jnp.clip(u, -SWIGLU_LIMIT, SWIGLU_LIMIT)
        act = (u + 1.0) * (g * jax.nn.sigmoid(SWIGLU_ALPHA * g))
        return act @ w_dn[e] + b_dn[e]

    y = lax.map(expert_block, (row_tok.reshape(n_blk, EXPERT_BLOCK), blk_e))
    y = y.reshape(n_rows, d).astype(jnp.float32) * row_gate[:, None]
    out = jax.ops.segment_sum(y, row_tok, num_segments=n + 1)[:n]
    return out.astype(x.dtype).reshape(b, s, d)


def setup_inputs(seed: int = 0) -> dict:
    key = jax.random.key(seed)
    ks = jax.random.split(key, 20)
    nrm = lambda k, shape, scale: jax.random.normal(k, shape, jnp.float32) * scale
    x = nrm(ks[0], (BATCH, SEQ, D_MODEL), 1.0)
    positions = jnp.tile(jnp.arange(SEQ, dtype=jnp.int32)[None, :], (BATCH, 1))
    ln_mix_g = 1.0 + nrm(ks[1], (DEPTH, D_MODEL), 0.02)
    ln_mix_b = nrm(ks[2], (DEPTH, D_MODEL), 0.02)
    ln_ffn_g = 1.0 + nrm(ks[3], (DEPTH, D_MODEL), 0.02)
    ln_ffn_b = nrm(ks[4], (DEPTH, D_MODEL), 0.02)
    even_w_in = nrm(ks[5], (N_EVEN, D_MODEL, EVEN_IN), D_MODEL ** -0.5)
    fox_b_f = 2.0 + nrm(ks[6], (N_EVEN, FOX_HEADS), 0.1)
    even_w_o = nrm(ks[7], (N_EVEN, EVEN_MIX, D_MODEL), EVEN_MIX ** -0.5 * DEEPNORM_BETA)
    mla_w_down = nrm(ks[8], (N_ODD, D_MODEL, MLA_DOWN), D_MODEL ** -0.5)
    mla_q_norm = 1.0 + nrm(ks[9], (N_ODD, Q_LORA), 0.02)
    mla_kv_norm = 1.0 + nrm(ks[10], (N_ODD, KV_LORA), 0.02)
    mla_w_uq = nrm(ks[11], (N_ODD, Q_LORA, MLA_HEADS * (NOPE_DIM + ROPE_DIM)), Q_LORA ** -0.5)
    mla_w_ukv = nrm(ks[12], (N_ODD, KV_LORA, MLA_HEADS * (NOPE_DIM + V_DIM)), KV_LORA ** -0.5)
    mla_w_o = nrm(ks[13], (N_ODD, MLA_HEADS * V_DIM, D_MODEL),
                  (MLA_HEADS * V_DIM) ** -0.5 * DEEPNORM_BETA)
    router_w = nrm(ks[14], (DEPTH, D_MODEL, N_EXPERTS), D_MODEL ** -0.5)
    router_b = nrm(ks[15], (DEPTH, N_EXPERTS), 0.01)
    expert_w_gate_up = nrm(ks[16], (DEPTH, N_EXPERTS, D_MODEL, 2 * D_EXPERT), D_MODEL ** -0.5)
    expert_b_gate_up = nrm(ks[17], (DEPTH, N_EXPERTS, 2 * D_EXPERT), 0.01)
    expert_w_down = nrm(ks[18], (DEPTH, N_EXPERTS, D_EXPERT, D_MODEL), D_EXPERT ** -0.5 * DEEPNORM_BETA)
    expert_b_down = nrm(ks[19], (DEPTH, N_EXPERTS, D_MODEL), 0.01)
    return {"x": x, "positions": positions,
            "ln_mix_g": ln_mix_g, "ln_mix_b": ln_mix_b, "ln_ffn_g": ln_ffn_g, "ln_ffn_b": ln_ffn_b,
            "even_w_in": even_w_in, "fox_b_f": fox_b_f, "even_w_o": even_w_o,
            "mla_w_down": mla_w_down, "mla_q_norm": mla_q_norm, "mla_kv_norm": mla_kv_norm,
            "mla_w_uq": mla_w_uq, "mla_w_ukv": mla_w_ukv, "mla_w_o": mla_w_o,
            "router_w": router_w, "router_b": router_b,
            "expert_w_gate_up": expert_w_gate_up, "expert_b_gate_up": expert_b_gate_up,
            "expert_w_down": expert_w_down, "expert_b_down": expert_b_down}


def reference(x, positions, ln_mix_g, ln_mix_b, ln_ffn_g, ln_ffn_b,
              even_w_in, fox_b_f, even_w_o,
              mla_w_down, mla_q_norm, mla_kv_norm, mla_w_uq, mla_w_ukv, mla_w_o,
              router_w, router_b, expert_w_gate_up, expert_b_gate_up,
              expert_w_down, expert_b_down):
    for layer in range(DEPTH):
        i = layer // 2
        if layer % 2 == 0:
            mix = _sb_fox_mixer(x, even_w_in[i], fox_b_f[i], even_w_o[i])
        else:
            mix = _mla_mixer(x, positions, mla_w_down[i], mla_q_norm[i], mla_kv_norm[i],
                             mla_w_uq[i], mla_w_ukv[i], mla_w_o[i])
        x = _layer_norm(DEEPNORM_ALPHA * x + mix, ln_mix_g[layer], ln_mix_b[layer])
        ffn = _moe(x, router_w[layer], router_b[layer], expert_w_gate_up[layer],
                   expert_b_gate_up[layer], expert_w_down[layer], expert_b_down[layer])
        x = _layer_norm(DEEPNORM_ALPHA * x + ffn, ln_ffn_g[layer], ln_ffn_b[layer])
    return x
```

```python
import functools
import math

import jax
import jax.numpy as jnp
from jax import lax
from jax.experimental import pallas as pl
from jax.experimental.pallas import tpu as pltpu

F32 = jnp.float32
BF16 = jnp.bfloat16

HEAD_DIM = 128
SB_HEADS = 8
FOX_HEADS = 8
MLA_HEADS = 16
Q_LORA = 512
KV_LORA = 512
NOPE_DIM = 128
ROPE_DIM = 64
V_DIM = 128
ROPE_THETA = 10000.0
N_EXPERTS = 32
TOP_K = 4
SWIGLU_LIMIT = 7.0
SWIGLU_ALPHA = 1.702
LN_EPS = 1e-5
RMS_EPS = 1e-6

LANES = 128
MASK_VALUE = -1e30
VMEM_LIMIT = 56 * 1024 * 1024

MM_TM = 1024
MM_TN = 512
SB_TQ = 256
SB_TK = 128
FA_TQ = 256
FA_TK = 256
ROW_TILE = 256
GATE_TILE = 512
EXPERT_BM = 512
COMBINE_TM = 128


def _params(sem):
    return pltpu.CompilerParams(dimension_semantics=sem, vmem_limit_bytes=VMEM_LIMIT)


def _pick_tile(n, prefs):
    for t in prefs:
        if n % t == 0:
            return t
    return n


def _mm_kernel(a_ref, w_ref, o_ref):
    o_ref[...] = jnp.dot(a_ref[...].astype(BF16), w_ref[...],
                         preferred_element_type=F32).astype(o_ref.dtype)


def _matmul(a, w, out_dtype):
    m, k = a.shape
    n = w.shape[1]
    tm = _pick_tile(m, (MM_TM, 512, 256, 128))
    tn = _pick_tile(n, (MM_TN, 384, 256, 128))
    return pl.pallas_call(
        _mm_kernel,
        out_shape=jax.ShapeDtypeStruct((m, n), out_dtype),
        grid=(m // tm, n // tn),
        in_specs=[pl.BlockSpec((tm, k), lambda i, j: (i, 0)),
                  pl.BlockSpec((k, tn), lambda i, j: (0, j))],
        out_specs=pl.BlockSpec((tm, tn), lambda i, j: (i, j)),
        compiler_params=_params(("parallel", "parallel")),
        name="matmul",
    )(a, w)


def _split2(x):
    hi = x.astype(BF16)
    lo = (x - hi.astype(F32)).astype(BF16)
    return hi, lo


def _split3(x):
    h1 = x.astype(BF16)
    r1 = x - h1.astype(F32)
    h2 = r1.astype(BF16)
    h3 = (r1 - h2.astype(F32)).astype(BF16)
    return h1, h2, h3


def _log1pexp_neg_abs(z):
    return jnp.log(1.0 + jnp.exp(-jnp.abs(z)))


def _layer_norm(h, g, b):
    mu = jnp.mean(h, axis=-1, keepdims=True)
    c = h - mu
    var = jnp.mean(c * c, axis=-1, keepdims=True)
    return c * lax.rsqrt(var + LN_EPS) * g + b


def _fox_gate_kernel(x_ref, w_ref, b_ref, o_ref, carry_ref):
    @pl.when(pl.program_id(1) == 0)
    def _():
        carry_ref[...] = jnp.zeros_like(carry_ref)

    f = jnp.dot(x_ref[...], w_ref[...], preferred_element_type=F32) + b_ref[...]
    log_f = jnp.minimum(f, 0.0) - _log1pexp_neg_abs(f)
    ts = log_f.shape[0]
    row = lax.broadcasted_iota(jnp.int32, (ts, ts), 0)
    col = lax.broadcasted_iota(jnp.int32, (ts, ts), 1)
    tri = jnp.where(col <= row, 1.0, 0.0).astype(BF16)
    cum = carry_ref[...]
    for part in _split3(log_f):
        cum = cum + jnp.dot(tri, part, preferred_element_type=F32)
    o_ref[...] = cum
    carry_ref[...] = cum[ts - 1:ts, :]


def _fox_gate(xb, w_f, b_f, batch, seq):
    n, d = xb.shape
    ts = _pick_tile(seq, (GATE_TILE, 256, 128))
    ns = seq // ts
    return pl.pallas_call(
        _fox_gate_kernel,
        out_shape=jax.ShapeDtypeStruct((n, LANES), F32),
        grid=(batch, ns),
        in_specs=[pl.BlockSpec((ts, d), lambda b, j: (b * ns + j, 0)),
                  pl.BlockSpec((d, LANES), lambda b, j: (0, 0)),
                  pl.BlockSpec((1, LANES), lambda b, j: (0, 0))],
        out_specs=pl.BlockSpec((ts, LANES), lambda b, j: (b * ns + j, 0)),
        scratch_shapes=[pltpu.VMEM((1, LANES), F32)],
        compiler_params=_params(("parallel", "arbitrary")),
        name="fox_gate",
    )(xb, w_f, b_f)


def _sb_kernel(q_ref, k_ref, v_ref, u_ref, o_ref, *, tq, tk, scale):
    qi = pl.program_id(2)
    q = q_ref[...]
    n_diag = tq // tk
    n_below = qi * n_diag
    q_pos = qi * tq + lax.broadcasted_iota(jnp.int32, (tq, tk), 0)
    k_iota = lax.broadcasted_iota(jnp.int32, (tq, tk), 1)

    def step(kb, acc, later_blocks, masked):
        start = pl.multiple_of(kb * tk, tk)
        k = k_ref[pl.ds(start, tk), :]
        v = v_ref[pl.ds(start, tk), :]
        z = lax.dot_general(q, k, (((1,), (1,)), ((), ())), preferred_element_type=F32) * scale
        tail = _log1pexp_neg_abs(z)
        sp = jnp.maximum(z, 0.0) + tail
        if masked:
            strict = (start + k_iota) < q_pos
            sp = jnp.where(strict, sp, 0.0)
        hi, lo = _split2(sp)
        sums = jnp.dot(jnp.concatenate([hi, lo], axis=1), u_ref[...], preferred_element_type=F32)
        later = sums[:, :tk] + later_blocks
        w = jnp.exp(jnp.minimum(z, 0.0) - tail - later)
        if masked:
            w = jnp.where(strict, w, 0.0)
        acc = acc + jnp.dot(w.astype(BF16), v, preferred_element_type=F32)
        return acc, later_blocks + sums[:, tk:]

    acc = jnp.zeros((tq, HEAD_DIM), F32)
    later_blocks = jnp.zeros((tq, tk), F32)
    for d in range(n_diag - 1, -1, -1):
        acc, later_blocks = step(n_below + d, acc, later_blocks, True)

    def body(i, carry):
        return step(n_below - 1 - i, carry[0], carry[1], False)

    acc, _ = lax.fori_loop(0, n_below, body, (acc, later_blocks))
    o_ref[...] = acc.astype(o_ref.dtype)


def _sb_attention(h, batch, seq, heads, q_col, k_col, v_col):
    n = h.shape[0]
    tq, tk = min(SB_TQ, seq), SB_TK
    nq = seq // tq
    j = lax.broadcasted_iota(jnp.int32, (2 * tk, 2 * tk), 0) % tk
    s = lax.broadcasted_iota(jnp.int32, (2 * tk, 2 * tk), 1)
    u = jnp.where((s >= tk) | (j > s), 1.0, 0.0).astype(BF16)
    kern = functools.partial(_sb_kernel, tq=tq, tk=tk, scale=HEAD_DIM ** -0.5)
    return pl.pallas_call(
        kern,
        out_shape=jax.ShapeDtypeStruct((n, heads * HEAD_DIM), BF16),
        grid=(batch, heads, nq),
        in_specs=[pl.BlockSpec((tq, HEAD_DIM), lambda b, hd, i: (b * nq + i, q_col + hd)),
                  pl.BlockSpec((seq, HEAD_DIM), lambda b, hd, i: (b, k_col + hd)),
                  pl.BlockSpec((seq, HEAD_DIM), lambda b, hd, i: (b, v_col + hd)),
                  pl.BlockSpec((2 * tk, 2 * tk), lambda b, hd, i: (0, 0))],
        out_specs=pl.BlockSpec((tq, HEAD_DIM), lambda b, hd, i: (b * nq + i, hd)),
        compiler_params=_params(("parallel", "parallel", "arbitrary")),
        name="sb_attention",
    )(h, h, h, u)


def _flash_sweep(q, k_ref, v_ref, qi, tq, tk, scale, bias_fn):
    n_diag = tq // tk
    n_below = qi * n_diag
    dv = v_ref.shape[-1]
    q_pos = qi * tq + lax.broadcasted_iota(jnp.int32, (tq, tk), 0)
    k_iota = lax.broadcasted_iota(jnp.int32, (tq, tk), 1)

    def step(kb, carry, masked):
        m, l, acc = carry
        start = pl.multiple_of(kb * tk, tk)
        k = k_ref[pl.ds(start, tk), :]
        v = v_ref[pl.ds(start, tk), :]
        s = lax.dot_general(q, k, (((1,), (1,)), ((), ())), preferred_element_type=F32) * scale
        if bias_fn is not None:
            s = s + bias_fn(kb)
        if masked:
            s = jnp.where((start + k_iota) <= q_pos, s, MASK_VALUE)
        m_new = jnp.maximum(m, jnp.max(s, axis=-1, keepdims=True))
        p = jnp.exp(s - m_new)
        corr = jnp.exp(m - m_new)
        l = corr * l + jnp.sum(p, axis=-1, keepdims=True)
        acc = corr * acc + jnp.dot(p.astype(BF16), v, preferred_element_type=F32)
        return m_new, l, acc

    carry = (jnp.full((tq, 1), MASK_VALUE, F32), jnp.zeros((tq, 1), F32), jnp.zeros((tq, dv), F32))
    carry = lax.fori_loop(0, n_below, lambda kb, c: step(kb, c, False), carry)
    for d in range(n_diag):
        carry = step(n_below + d, carry, True)
    _, l, acc = carry
    return acc / l


def _fox_kernel(q_ref, k_ref, v_ref, cq_ref, ck_ref, o_ref, *, tq, tk, scale):
    hd = pl.program_id(1)
    qi = pl.program_id(2)
    lane = lax.broadcasted_iota(jnp.int32, cq_ref.shape, 1)
    cum_q = jnp.sum(jnp.where(lane == hd, cq_ref[...], 0.0), axis=-1, keepdims=True)

    def bias(kb):
        return cum_q - ck_ref[pl.ds(kb, 1), :]

    o = _flash_sweep(q_ref[...], k_ref, v_ref, qi, tq, tk, scale, bias)
    o_ref[...] = o.astype(o_ref.dtype)


def _fox_attention(h, cum, batch, seq, heads, q_col, k_col, v_col):
    n = h.shape[0]
    tq, tk = min(FA_TQ, seq), min(FA_TK, seq)
    nq, nk = seq // tq, seq // tk
    cum_k = jnp.transpose(cum[:, :heads].reshape(batch, seq, heads), (0, 2, 1)).reshape(batch, heads, nk, tk)
    kern = functools.partial(_fox_kernel, tq=tq, tk=tk, scale=HEAD_DIM ** -0.5)
    return pl.pallas_call(
        kern,
        out_shape=jax.ShapeDtypeStruct((n, heads * HEAD_DIM), BF16),
        grid=(batch, heads, nq),
        in_specs=[pl.BlockSpec((tq, HEAD_DIM), lambda b, hd, i: (b * nq + i, q_col + hd)),
                  pl.BlockSpec((seq, HEAD_DIM), lambda b, hd, i: (b, k_col + hd)),
                  pl.BlockSpec((seq, HEAD_DIM), lambda b, hd, i: (b, v_col + hd)),
                  pl.BlockSpec((tq, LANES), lambda b, hd, i: (b * nq + i, 0)),
                  pl.BlockSpec((None, None, nk, tk), lambda b, hd, i: (b, hd, 0, 0))],
        out_specs=pl.BlockSpec((tq, HEAD_DIM), lambda b, hd, i: (b * nq + i, hd)),
        compiler_params=_params(("parallel", "parallel", "arbitrary")),
        name="fox_attention",
    )(h, h, h, cum, cum_k)


def _rope_lanes(t, ra, rb, rc):
    half = ROPE_DIM // 2
    return t * ra + pltpu.roll(t, half, 1) * rb + pltpu.roll(t, LANES - half, 1) * rc


def _mla_kernel(q_ref, kn_ref, kp_ref, v_ref, ra_ref, rb_ref, rc_ref, o_ref, kfull_ref, *, tq, tk, scale):
    qi = pl.program_id(2)

    @pl.when(qi == 0)
    def _():
        kfull_ref[:, :NOPE_DIM] = kn_ref[...]
        kfull_ref[:, NOPE_DIM:] = kp_ref[...]

    q = q_ref[...]
    q_pe = _rope_lanes(q[:, NOPE_DIM:].astype(F32), ra_ref[...], rb_ref[...], rc_ref[...])
    q_full = jnp.concatenate([q[:, :NOPE_DIM], q_pe.astype(BF16)], axis=1)
    o = _flash_sweep(q_full, kfull_ref, v_ref, qi, tq, tk, scale, None)
    o_ref[...] = o.astype(o_ref.dtype)


def _mla_attention(q, kv, k_pe, ra, rb, rc, batch, seq, heads):
    n = q.shape[0]
    tq, tk = min(FA_TQ, seq), min(FA_TK, seq)
    nq = seq // tq
    dq = NOPE_DIM + LANES
    kern = functools.partial(_mla_kernel, tq=tq, tk=tk, scale=(NOPE_DIM + ROPE_DIM) ** -0.5)
    rope_spec = pl.BlockSpec((tq, LANES), lambda b, hd, i: (b * nq + i, 0))
    return pl.pallas_call(
        kern,
        out_shape=jax.ShapeDtypeStruct((n, heads * V_DIM), BF16),
        grid=(batch, heads, nq),
        in_specs=[pl.BlockSpec((tq, dq), lambda b, hd, i: (b * nq + i, hd)),
                  pl.BlockSpec((seq, NOPE_DIM), lambda b, hd, i: (b, hd)),
                  pl.BlockSpec((seq, LANES), lambda b, hd, i: (b, 0)),
                  pl.BlockSpec((seq, V_DIM), lambda b, hd, i: (b, heads + hd)),
                  rope_spec, rope_spec, rope_spec],
        out_specs=pl.BlockSpec((tq, V_DIM), lambda b, hd, i: (b * nq + i, hd)),
        scratch_shapes=[pltpu.VMEM((seq, dq), BF16)],
        compiler_params=_params(("parallel", "parallel", "arbitrary")),
        name="mla_attention",
    )(q, kv, k_pe, kv, ra, rb, rc)


def _mla_prep_kernel(down_ref, pos_ref, qn_ref, kvn_ref, invf_ref,
                     cq_ref, ckv_ref, kpe_ref, ra_ref, rb_ref, rc_ref):
    down = down_ref[...]

    def rms(c, g):
        return c * lax.rsqrt(jnp.mean(c * c, axis=-1, keepdims=True) + RMS_EPS) * g

    cq_ref[...] = rms(down[:, :Q_LORA], qn_ref[...]).astype(cq_ref.dtype)
    ckv_ref[...] = rms(down[:, Q_LORA:Q_LORA + KV_LORA], kvn_ref[...]).astype(ckv_ref.dtype)

    ang = pos_ref[...].astype(F32) * invf_ref[...]
    cos, sin = jnp.cos(ang), jnp.sin(ang)
    lane = lax.broadcasted_iota(jnp.int32, ang.shape, 1)
    half = ROPE_DIM // 2
    ra = jnp.where(lane < ROPE_DIM, cos, 0.0)
    rb = jnp.where((lane >= half) & (lane < ROPE_DIM), sin, 0.0)
    rc = jnp.where(lane < half, -sin, 0.0)
    ra_ref[...] = ra
    rb_ref[...] = rb
    rc_ref[...] = rc
    kpe_ref[...] = _rope_lanes(down[:, Q_LORA + KV_LORA:], ra, rb, rc).astype(kpe_ref.dtype)


def _mla_prep(down, pos, q_norm, kv_norm, inv_freq):
    n, dd = down.shape
    tm = _pick_tile(n, (ROW_TILE, 128))
    row = lambda w: pl.BlockSpec((tm, w), lambda i: (i, 0))
    const = lambda w: pl.BlockSpec((1, w), lambda i: (0, 0))
    return pl.pallas_call(
        _mla_prep_kernel,
        out_shape=(jax.ShapeDtypeStruct((n, Q_LORA), BF16),
                   jax.ShapeDtypeStruct((n, KV_LORA), BF16),
                   jax.ShapeDtypeStruct((n, LANES), BF16),
                   jax.ShapeDtypeStruct((n, LANES), F32),
                   jax.ShapeDtypeStruct((n, LANES), F32),
                   jax.ShapeDtypeStruct((n, LANES), F32)),
        grid=(n // tm,),
        in_specs=[row(dd), row(1), const(Q_LORA), const(KV_LORA), const(LANES)],
        out_specs=(row(Q_LORA), row(KV_LORA), row(LANES), row(LANES), row(LANES), row(LANES)),
        compiler_params=_params(("parallel",)),
        name="mla_prep",
    )(down, pos, q_norm, kv_norm, inv_freq)


def _ln_router_kernel(x_ref, mix_ref, g_ref, b_ref, rw_hi_ref, rw_lo_ref, rb_ref,
                      x1_ref, e_ref, gate_ref, rank_ref, cnt_ref, carry_ref, *, alpha):
    @pl.when(pl.program_id(0) == 0)
    def _():
        carry_ref[...] = jnp.zeros_like(carry_ref)

    x1 = _layer_norm(alpha * x_ref[...] + mix_ref[...], g_ref[...], b_ref[...])
    x1_ref[...] = x1

    a_hi, a_lo = _split2(x1)
    rw_hi, rw_lo = rw_hi_ref[...], rw_lo_ref[...]
    logits = (jnp.dot(a_hi, rw_hi, preferred_element_type=F32)
              + jnp.dot(a_hi, rw_lo, preferred_element_type=F32)
              + jnp.dot(a_lo, rw_hi, preferred_element_type=F32)) + rb_ref[...]

    tm = logits.shape[0]
    lane = lax.broadcasted_iota(jnp.int32, (tm, LANES), 1)
    work = logits
    picks, vals = [], []
    for _ in range(TOP_K):
        mx = jnp.max(work, axis=-1, keepdims=True)
        idx = jnp.min(jnp.where(work == mx, lane, LANES), axis=-1, keepdims=True)
        pick = lane == idx
        picks.append(pick)
        vals.append(mx)
        work = jnp.where(pick, -jnp.inf, work)

    exps = [jnp.exp(v - vals[0]) for v in vals]
    denom = exps[0]
    for e in exps[1:]:
        denom = denom + e

    chosen = jnp.zeros((tm, LANES), F32)
    for pick in picks:
        chosen = chosen + jnp.where(pick, 1.0, 0.0)
    row = lax.broadcasted_iota(jnp.int32, (tm, tm), 0)
    col = lax.broadcasted_iota(jnp.int32, (tm, tm), 1)
    strict_lower = jnp.where(col < row, 1.0, 0.0).astype(BF16)
    before = jnp.dot(strict_lower, chosen.astype(BF16), preferred_element_type=F32) + carry_ref[...]

    e_out = jnp.zeros((tm, LANES), jnp.int32)
    gate_out = jnp.zeros((tm, LANES), F32)
    rank_out = jnp.zeros((tm, LANES), F32)
    for k in range(TOP_K):
        idx = jnp.min(jnp.where(picks[k], lane, LANES), axis=-1, keepdims=True)
        rank = jnp.sum(jnp.where(picks[k], before, 0.0), axis=-1, keepdims=True)
        e_out = jnp.where(lane == k, idx, e_out)
        gate_out = jnp.where(lane == k, exps[k] / denom, gate_out)
        rank_out = jnp.where(lane == k, rank, rank_out)
    e_ref[...] = e_out
    gate_ref[...] = gate_out
    rank_ref[...] = rank_out.astype(jnp.int32)

    carry_ref[...] = carry_ref[...] + jnp.sum(chosen, axis=0, keepdims=True)
    cnt_ref[...] = carry_ref[...]


def _ln_router(x, mix, g, b, rw_hi, rw_lo, rb, alpha):
    n, d = x.shape
    tm = _pick_tile(n, (ROW_TILE, 128))
    row = lambda w: pl.BlockSpec((tm, w), lambda i: (i, 0))
    const = lambda r, w: pl.BlockSpec((r, w), lambda i: (0, 0))
    return pl.pallas_call(
        functools.partial(_ln_router_kernel, alpha=alpha),
        out_shape=(jax.ShapeDtypeStruct((n, d), F32),
                   jax.ShapeDtypeStruct((n, LANES), jnp.int32),
                   jax.ShapeDtypeStruct((n, LANES), F32),
                   jax.ShapeDtypeStruct((n, LANES), jnp.int32),
                   jax.ShapeDtypeStruct((1, LANES), F32)),
        grid=(n // tm,),
        in_specs=[row(d), row(d), const(1, d), const(1, d), const(d, LANES), const(d, LANES),
                  const(1, LANES)],
        out_specs=(row(d), row(LANES), row(LANES), row(LANES), const(1, LANES)),
        scratch_shapes=[pltpu.VMEM((1, LANES), F32)],
        compiler_params=_params(("arbitrary",)),
        name="ln_router",
    )(x, mix, g, b, rw_hi, rw_lo, rb)


def _expert_kernel(blk_e_ref, n_used_ref, row_tok_ref,
                   x_hbm, wgu_ref, bgu_ref, wdn_ref, bdn_ref, y_ref, xbuf, sem, *, bm, d_expert):
    i = pl.program_id(0)
    n_used = n_used_ref[0]

    def gather_copy(tok, r, slot):
        return pltpu.make_async_copy(x_hbm.at[pl.ds(tok, 1)], xbuf.at[slot, pl.ds(r, 1)], sem.at[slot])

    def issue(blk, slot):
        def body(r, _):
            gather_copy(row_tok_ref[blk * bm + r], r, slot).start()
            return 0
        lax.fori_loop(0, bm, body, 0, unroll=8)

    @pl.when(jnp.logical_and(i == 0, n_used > 0))
    def _():
        issue(0, 0)

    @pl.when(i + 1 < n_used)
    def _():
        issue(i + 1, (i + 1) % 2)

    @pl.when(i < n_used)
    def _():
        slot = i % 2

        def wait_body(r, _):
            gather_copy(0, r, slot).wait()
            return 0
        lax.fori_loop(0, bm, wait_body, 0, unroll=8)

        xb = xbuf[slot].astype(BF16)
        hb = jnp.dot(xb, wgu_ref[...], preferred_element_type=F32) + bgu_ref[...]
        g = jnp.minimum(hb[:, :d_expert], SWIGLU_LIMIT)
        u = jnp.clip(hb[:, d_expert:], -SWIGLU_LIMIT, SWIGLU_LIMIT)
        act = (u + 1.0) * (g * (1.0 / (1.0 + jnp.exp(-SWIGLU_ALPHA * g))))
        y_ref[...] = jnp.dot(act.astype(BF16), wdn_ref[...], preferred_element_type=F32) + bdn_ref[...]

    @pl.when(i >= n_used)
    def _():
        y_ref[...] = jnp.zeros_like(y_ref)


def _experts(x1, blk_e, n_used, row_tok, w_gu, b_gu, w_dn, b_dn, bm):
    n, d = x1.shape
    n_exp, _, f2 = w_gu.shape
    d_expert = f2 // 2
    n_rows = row_tok.shape[0]
    n_blk = n_rows // bm
    grid_spec = pltpu.PrefetchScalarGridSpec(
        num_scalar_prefetch=3,
        grid=(n_blk,),
        in_specs=[pl.BlockSpec(memory_space=pl.ANY),
                  pl.BlockSpec((None, d, f2), lambda i, be, nu, rt: (be[i], 0, 0)),
                  pl.BlockSpec((None, 1, f2), lambda i, be, nu, rt: (be[i], 0, 0)),
                  pl.BlockSpec((None, d_expert, d), lambda i, be, nu, rt: (be[i], 0, 0)),
                  pl.BlockSpec((None, 1, d), lambda i, be, nu, rt: (be[i], 0, 0))],
        out_specs=pl.BlockSpec((bm, d), lambda i, be, nu, rt: (i, 0)),
        scratch_shapes=[pltpu.VMEM((2, bm, d), F32), pltpu.SemaphoreType.DMA((2,))],
    )
    return pl.pallas_call(
        functools.partial(_expert_kernel, bm=bm, d_expert=d_expert),
        out_shape=jax.ShapeDtypeStruct((n_rows, d), F32),
        grid_spec=grid_spec,
        compiler_params=_params(("arbitrary",)),
        name="experts",
    )(blk_e, n_used, row_tok, x1, w_gu, b_gu.reshape(n_exp, 1, f2), w_dn, b_dn.reshape(n_exp, 1, d))


def _combine_kernel(pos_ref, y_hbm, x1_ref, gate_ref, g_ref, b_ref, x2_ref, xb_ref, ybuf, sem, *, tm, alpha):
    i = pl.program_id(0)
    n_steps = pl.num_programs(0)

    def gather_copy(src_row, k, t, slot):
        return pltpu.make_async_copy(y_hbm.at[pl.ds(src_row, 1)], ybuf.at[slot, k, pl.ds(t, 1)], sem.at[slot])

    def issue(step, slot):
        def body(t, _):
            for k in range(TOP_K):
                gather_copy(pos_ref[(step * tm + t) * TOP_K + k], k, t, slot).start()
            return 0
        lax.fori_loop(0, tm, body, 0, unroll=2)

    @pl.when(i == 0)
    def _():
        issue(0, 0)

    @pl.when(i + 1 < n_steps)
    def _():
        issue(i + 1, (i + 1) % 2)

    slot = i % 2

    def wait_body(t, _):
        for k in range(TOP_K):
            gather_copy(0, k, t, slot).wait()
        return 0
    lax.fori_loop(0, tm, wait_body, 0, unroll=2)

    gate = gate_ref[...]
    ffn = gate[:, 0:1] * ybuf[slot, 0]
    for k in range(1, TOP_K):
        ffn = ffn + gate[:, k:k + 1] * ybuf[slot, k]
    x2 = _layer_norm(alpha * x1_ref[...] + ffn, g_ref[...], b_ref[...])
    x2_ref[...] = x2
    xb_ref[...] = x2.astype(BF16)


def _combine_ln(pos_flat, y, x1, gate, g, b, alpha):
    n, d = x1.shape
    tm = _pick_tile(n, (COMBINE_TM,))
    grid_spec = pltpu.PrefetchScalarGridSpec(
        num_scalar_prefetch=1,
        grid=(n // tm,),
        in_specs=[pl.BlockSpec(memory_space=pl.ANY),
                  pl.BlockSpec((tm, d), lambda i, p: (i, 0)),
                  pl.BlockSpec((tm, LANES), lambda i, p: (i, 0)),
                  pl.BlockSpec((1, d), lambda i, p: (0, 0)),
                  pl.BlockSpec((1, d), lambda i, p: (0, 0))],
        out_specs=(pl.BlockSpec((tm, d), lambda i, p: (i, 0)),
                   pl.BlockSpec((tm, d), lambda i, p: (i, 0))),
        scratch_shapes=[pltpu.VMEM((2, TOP_K, tm, d), F32), pltpu.SemaphoreType.DMA((2,))],
    )
    return pl.pallas_call(
        functools.partial(_combine_kernel, tm=tm, alpha=alpha),
        out_shape=(jax.ShapeDtypeStruct((n, d), F32), jax.ShapeDtypeStruct((n, d), BF16)),
        grid_spec=grid_spec,
        compiler_params=_params(("arbitrary",)),
        name="combine_ln",
    )(pos_flat, y, x1, gate, g, b)


def _pad_cols(w, width):
    return jnp.pad(w, ((0, 0), (0, width - w.shape[1])))


def _moe_layer(x, mix, ln_g, ln_b, router_w, router_b, w_gu, b_gu, w_dn, b_dn, ln2_g, ln2_b, alpha):
    n, d = x.shape
    n_exp = router_w.shape[1]
    bm = EXPERT_BM

    rw = _pad_cols(router_w, LANES)
    rw_hi = rw.astype(BF16)
    rw_lo = (rw - rw_hi.astype(F32)).astype(BF16)
    rb = jnp.full((1, LANES), MASK_VALUE, F32).at[0, :n_exp].set(router_b)
    x1, top_e, gate, rank, cnt = _ln_router(x, mix, ln_g[None], ln_b[None], rw_hi, rw_lo, rb, alpha)

    counts = cnt[0, :n_exp].astype(jnp.int32)
    padded = (counts + bm - 1) // bm * bm
    pend = jnp.cumsum(padded)
    pstart = pend - padded
    top_e = top_e[:, :TOP_K]
    pos = (pstart[top_e] + rank[:, :TOP_K]).reshape(-1)
    n_rows = -(-(n * TOP_K + n_exp * (bm - 1)) // bm) * bm
    n_blk = n_rows // bm
    tok = jnp.repeat(jnp.arange(n, dtype=jnp.int32), TOP_K)
    row_tok = jnp.zeros((n_rows,), jnp.int32).at[pos].set(tok)
    blk_e = jnp.minimum(jnp.searchsorted(pend, jnp.arange(n_blk, dtype=jnp.int32) * bm, side='right'),
                        n_exp - 1).astype(jnp.int32)
    n_used = (pend[-1:] // bm).astype(jnp.int32)

    y = _experts(x1, blk_e, n_used, row_tok, w_gu.astype(BF16), b_gu, w_dn.astype(BF16), b_dn, bm)
    return _combine_ln(pos, y, x1, gate, ln2_g[None], ln2_b[None], alpha)


def _even_mixer(xb, w_in, b_f, w_o, batch, seq):
    sbw = SB_HEADS * HEAD_DIM
    fxw = FOX_HEADS * HEAD_DIM
    main = 3 * sbw + 3 * fxw
    h = _matmul(xb, w_in[:, :main].astype(BF16), BF16)
    w_f = _pad_cols(w_in[:, main:], LANES).astype(BF16)
    cum = _fox_gate(xb, w_f, _pad_cols(b_f[None], LANES), batch, seq)
    o_sb = _sb_attention(h, batch, seq, SB_HEADS, 0, SB_HEADS, 2 * SB_HEADS)
    c0 = 3 * SB_HEADS
    o_fx = _fox_attention(h, cum, batch, seq, FOX_HEADS, c0, c0 + FOX_HEADS, c0 + 2 * FOX_HEADS)
    o = jnp.concatenate([o_sb, o_fx], axis=-1)
    return _matmul(o, w_o.astype(BF16), F32)


def _mla_mixer(xb, pos, w_down, q_norm, kv_norm, w_uq, w_ukv, w_o, batch, seq):
    d = xb.shape[1]
    hds = MLA_HEADS
    w_down_p = _pad_cols(w_down, Q_LORA + KV_LORA + LANES).astype(BF16)
    down = _matmul(xb, w_down_p, F32)
    half = ROPE_DIM // 2
    freq = ROPE_THETA ** (-jnp.arange(0, ROPE_DIM, 2, dtype=F32) / ROPE_DIM)
    inv_freq = jnp.concatenate([freq, freq, jnp.zeros((LANES - ROPE_DIM,), F32)])[None]
    cq, ckv, k_pe, ra, rb, rc = _mla_prep(down, pos, q_norm[None], kv_norm[None], inv_freq)

    wq = w_uq.reshape(Q_LORA, hds, NOPE_DIM + ROPE_DIM)
    wq = jnp.pad(wq, ((0, 0), (0, 0), (0, LANES - ROPE_DIM))).reshape(Q_LORA, hds * (NOPE_DIM + LANES))
    wkv = w_ukv.reshape(KV_LORA, hds, NOPE_DIM + V_DIM)
    wkv = jnp.concatenate([wkv[:, :, :NOPE_DIM].reshape(KV_LORA, hds * NOPE_DIM),
                           wkv[:, :, NOPE_DIM:].reshape(KV_LORA, hds * V_DIM)], axis=1)
    q = _matmul(cq, wq.astype(BF16), BF16)
    kv = _matmul(ckv, wkv.astype(BF16), BF16)
    o = _mla_attention(q, kv, k_pe, ra, rb, rc, batch, seq, hds)
    return _matmul(o, w_o.astype(BF16), F32)


def kernel(x, positions, ln_mix_g, ln_mix_b, ln_ffn_g, ln_ffn_b, even_w_in, fox_b_f, even_w_o, mla_w_down, mla_q_norm, mla_kv_norm, mla_w_uq, mla_w_ukv, mla_w_o, router_w, router_b, expert_w_gate_up, expert_b_gate_up, expert_w_down, expert_b_down):
    batch, seq, d = x.shape
    depth = ln_mix_g.shape[0]
    alpha = (2 * depth) ** 0.25
    n = batch * seq
    xf = x.reshape(n, d)
    xb = xf.astype(BF16)
    pos = positions.reshape(n, 1)
    for layer in range(depth):
        i = layer // 2
        if layer % 2 == 0:
            mix = _even_mixer(xb, even_w_in[i], fox_b_f[i], even_w_o[i], batch, seq)
        else:
            mix = _mla_mixer(xb, pos, mla_w_down[i], mla_q_norm[i], mla_kv_norm[i],
                             mla_w_uq[i], mla_w_ukv[i], mla_w_o[i], batch, seq)
        xf, xb = _moe_layer(xf, mix, ln_mix_g[layer], ln_mix_b[layer], router_w[layer], router_b[layer],
                            expert_w_gate_up[layer], expert_b_gate_up[layer],
                            expert_w_down[layer], expert_b_down[layer],
                            ln_ffn_g[layer], ln_ffn_b[layer], alpha)
    return xf.reshape(batch, seq, d)
```

```python
import functools
import math

import jax
import jax.numpy as jnp
from jax import lax
from jax.experimental import pallas as pl
from jax.experimental.pallas import tpu as pltpu

F32 = jnp.float32
BF16 = jnp.bfloat16

HEAD_DIM = 128
SB_HEADS = 8
FOX_HEADS = 8
MLA_HEADS = 16
Q_LORA = 512
KV_LORA = 512
NOPE_DIM = 128
ROPE_DIM = 64
V_DIM = 128
ROPE_THETA = 10000.0
N_EXPERTS = 32
TOP_K = 4
SWIGLU_LIMIT = 7.0
SWIGLU_ALPHA = 1.702
LN_EPS = 1e-5
RMS_EPS = 1e-6

LANES = 128
MASK_VALUE = -1e30
VMEM_LIMIT = 56 * 1024 * 1024
LOG2E = 1.4426950408889634
LN2 = 0.6931471805599453

MM_TM = 1024
MM_TN = 512
SB_TQ = 512
SB_CK = 256
SB_GROUP = 2
FA_TQ = 512
FA_TK = 512
FA_GROUP = 2
ROW_TILE = 256
GATE_TILE = 512
EXPERT_BM = 512
COMBINE_TM = 128


def _params(sem):
    return pltpu.CompilerParams(dimension_semantics=sem, vmem_limit_bytes=VMEM_LIMIT)


def _pick_tile(n, prefs):
    for t in prefs:
        if n % t == 0:
            return t
    return n


def _mm_kernel(a_ref, w_ref, o_ref):
    o_ref[...] = jnp.dot(a_ref[...].astype(BF16), w_ref[...],
                         preferred_element_type=F32).astype(o_ref.dtype)


def _matmul(a, w, out_dtype):
    m, k = a.shape
    n = w.shape[1]
    tm = _pick_tile(m, (MM_TM, 512, 256, 128))
    tn = _pick_tile(n, (MM_TN, 384, 256, 128))
    return pl.pallas_call(
        _mm_kernel,
        out_shape=jax.ShapeDtypeStruct((m, n), out_dtype),
        grid=(m // tm, n // tn),
        in_specs=[pl.BlockSpec((tm, k), lambda i, j: (i, 0)),
                  pl.BlockSpec((k, tn), lambda i, j: (0, j))],
        out_specs=pl.BlockSpec((tm, tn), lambda i, j: (i, j)),
        compiler_params=_params(("parallel", "parallel")),
        name="matmul",
    )(a, w)


def _split2(x):
    hi = x.astype(BF16)
    lo = (x - hi.astype(F32)).astype(BF16)
    return hi, lo


def _split3(x):
    h1 = x.astype(BF16)
    r1 = x - h1.astype(F32)
    h2 = r1.astype(BF16)
    h3 = (r1 - h2.astype(F32)).astype(BF16)
    return h1, h2, h3


def _log1pexp_neg_abs(z):
    return jnp.log(1.0 + jnp.exp(-jnp.abs(z)))


def _layer_norm(h, g, b):
    mu = jnp.mean(h, axis=-1, keepdims=True)
    c = h - mu
    var = jnp.mean(c * c, axis=-1, keepdims=True)
    return c * lax.rsqrt(var + LN_EPS) * g + b


def _fox_gate_kernel(x_ref, w_ref, b_ref, o_ref, carry_ref):
    @pl.when(pl.program_id(1) == 0)
    def _():
        carry_ref[...] = jnp.zeros_like(carry_ref)

    f = jnp.dot(x_ref[...], w_ref[...], preferred_element_type=F32) + b_ref[...]
    log_f = jnp.minimum(f, 0.0) - _log1pexp_neg_abs(f)
    ts = log_f.shape[0]
    row = lax.broadcasted_iota(jnp.int32, (ts, ts), 0)
    col = lax.broadcasted_iota(jnp.int32, (ts, ts), 1)
    tri = jnp.where(col <= row, 1.0, 0.0).astype(BF16)
    cum = carry_ref[...]
    for part in _split3(log_f):
        cum = cum + jnp.dot(tri, part, preferred_element_type=F32)
    o_ref[...] = cum
    carry_ref[...] = cum[ts - 1:ts, :]


def _fox_gate(xb, w_f, b_f, batch, seq):
    n, d = xb.shape
    ts = _pick_tile(seq, (GATE_TILE, 256, 128))
    ns = seq // ts
    return pl.pallas_call(
        _fox_gate_kernel,
        out_shape=jax.ShapeDtypeStruct((n, LANES), F32),
        grid=(batch, ns),
        in_specs=[pl.BlockSpec((ts, d), lambda b, j: (b * ns + j, 0)),
                  pl.BlockSpec((d, LANES), lambda b, j: (0, 0)),
                  pl.BlockSpec((1, LANES), lambda b, j: (0, 0))],
        out_specs=pl.BlockSpec((ts, LANES), lambda b, j: (b * ns + j, 0)),
        scratch_shapes=[pltpu.VMEM((1, LANES), F32)],
        compiler_params=_params(("parallel", "arbitrary")),
        name="fox_gate",
    )(xb, w_f, b_f)


def _head_cols(hd):
    return slice(hd * HEAD_DIM, (hd + 1) * HEAD_DIM)


def _sb_kernel(q_ref, k_ref, v_ref, u_ref, o_ref, *, tq, ck, group, zscale):
    qi = pl.program_id(2)
    n_diag = tq // ck
    n_below = qi * n_diag

    def chunk(q, hd, kb, later_chunks, row0, masked):
        rows = q.shape[0]
        start = pl.multiple_of(kb * ck, ck)
        k = k_ref[pl.ds(start, ck), _head_cols(hd)]
        v = v_ref[pl.ds(start, ck), _head_cols(hd)]
        z = lax.dot_general(q, k, (((1,), (1,)), ((), ())), preferred_element_type=F32) * zscale
        sp = jnp.maximum(z, 0.0) + jnp.log(1.0 + jnp.exp2(-jnp.abs(z))) * (1.0 / LN2)
        sp_sum = sp
        if masked:
            q_pos = qi * tq + row0 + lax.broadcasted_iota(jnp.int32, (rows, ck), 0)
            strict = (start + lax.broadcasted_iota(jnp.int32, (rows, ck), 1)) < q_pos
            sp_sum = jnp.where(strict, sp, 0.0)
        hi, lo = _split2(sp_sum)
        later = jnp.dot(jnp.concatenate([hi, lo], axis=1), u_ref[...],
                        preferred_element_type=F32) + later_chunks
        w = jnp.exp2(z - sp - later)
        if masked:
            w = jnp.where(strict, w, 0.0)
        pv = jnp.dot(w.astype(BF16), v, preferred_element_type=F32)
        return pv, later_chunks + jnp.sum(sp_sum, axis=-1, keepdims=True)

    accs, laters = [], []
    for hd in range(group):
        q = q_ref[:, _head_cols(hd)]
        acc = jnp.zeros((tq, HEAD_DIM), F32)
        later = jnp.zeros((tq, 1), F32)
        for d in range(n_diag - 1, -1, -1):
            r0 = d * ck
            pv, later_s = chunk(q[r0:], hd, n_below + d, later[r0:], r0, True)
            acc_s = acc[r0:] + pv
            acc = jnp.concatenate([acc[:r0], acc_s], axis=0) if r0 else acc_s
            later = jnp.concatenate([later[:r0], later_s], axis=0) if r0 else later_s
        accs.append(acc)
        laters.append(later)

    def body(i, carry):
        accs, laters = carry
        kb = n_below - 1 - 2 * i
        new_accs, new_laters = [], []
        for hd in range(group):
            q = q_ref[:, _head_cols(hd)]
            pv1, later = chunk(q, hd, kb, laters[hd], 0, False)
            pv0, later = chunk(q, hd, kb - 1, later, 0, False)
            new_accs.append(accs[hd] + pv1 + pv0)
            new_laters.append(later)
        return tuple(new_accs), tuple(new_laters)

    accs, _ = lax.fori_loop(0, n_below // 2, body, (tuple(accs), tuple(laters)))
    for hd in range(group):
        o_ref[:, _head_cols(hd)] = accs[hd].astype(o_ref.dtype)


def _sb_attention(h, batch, seq, heads, q_col, k_col, v_col):
    n = h.shape[0]
    tq, ck = min(SB_TQ, seq), min(SB_CK, seq // 2)
    nq = seq // tq
    group = SB_GROUP
    assert (tq // ck) % 2 == 0 and heads % group == 0
    assert q_col % group == 0 and k_col % group == 0 and v_col % group == 0
    gw = group * HEAD_DIM
    j = lax.broadcasted_iota(jnp.int32, (2 * ck, ck), 0) % ck
    s = lax.broadcasted_iota(jnp.int32, (2 * ck, ck), 1)
    u = jnp.where(j > s, 1.0, 0.0).astype(BF16)
    kern = functools.partial(_sb_kernel, tq=tq, ck=ck, group=group, zscale=HEAD_DIM ** -0.5 * LOG2E)
    return pl.pallas_call(
        kern,
        out_shape=jax.ShapeDtypeStruct((n, heads * HEAD_DIM), BF16),
        grid=(batch, heads // group, nq),
        in_specs=[pl.BlockSpec((tq, gw), lambda b, g, i: (b * nq + i, q_col // group + g)),
                  pl.BlockSpec((seq, gw), lambda b, g, i: (b, k_col // group + g)),
                  pl.BlockSpec((seq, gw), lambda b, g, i: (b, v_col // group + g)),
                  pl.BlockSpec((2 * ck, ck), lambda b, g, i: (0, 0))],
        out_specs=pl.BlockSpec((tq, gw), lambda b, g, i: (b * nq + i, g)),
        compiler_params=_params(("parallel", "parallel", "arbitrary")),
        name="sb_attention",
    )(h, h, h, u)


def _flash_sweep(qs, k_fn, v_fn, bias_fn, qi, tq, tk, zscale, dv):
    n_diag = tq // tk
    n_below = qi * n_diag

    def step(hd, kb, state, row0, masked):
        m, l, acc = state
        q = qs[hd][row0:]
        rows = q.shape[0]
        start = pl.multiple_of(kb * tk, tk)
        s = lax.dot_general(q, k_fn(hd, start), (((1,), (1,)), ((), ())),
                            preferred_element_type=F32) * zscale
        if bias_fn is not None:
            s = s + bias_fn(hd, kb)
        if masked:
            q_pos = qi * tq + row0 + lax.broadcasted_iota(jnp.int32, (rows, tk), 0)
            k_pos = start + lax.broadcasted_iota(jnp.int32, (rows, tk), 1)
            s = jnp.where(k_pos <= q_pos, s, MASK_VALUE)
        m_new = jnp.maximum(m, jnp.max(s, axis=-1, keepdims=True))
        p = jnp.exp2(s - m_new)
        corr = jnp.exp2(m - m_new)
        l = corr * l + jnp.sum(p, axis=-1, keepdims=True)
        acc = corr * acc + jnp.dot(p.astype(BF16), v_fn(hd, start), preferred_element_type=F32)
        return m_new, l, acc

    heads = range(len(qs))
    init = (jnp.full((tq, 1), MASK_VALUE, F32), jnp.zeros((tq, 1), F32), jnp.zeros((tq, dv), F32))
    states = lax.fori_loop(
        0, n_below, lambda kb, st: tuple(step(hd, kb, st[hd], 0, False) for hd in heads),
        tuple(init for _ in heads))
    outs = []
    for hd in heads:
        state = states[hd]
        for d in range(n_diag):
            r0 = d * tk
            new = step(hd, n_below + d, tuple(t[r0:] for t in state), r0, True)
            state = tuple(jnp.concatenate([t[:r0], t_new], axis=0) if r0 else t_new
                          for t, t_new in zip(state, new))
        _, l, acc = state
        outs.append(acc / l)
    return outs


def _fox_kernel(q_ref, k_ref, v_ref, ck_ref, o_ref, *, tq, tk, group, zscale):
    qs = [q_ref[:, _head_cols(hd)] for hd in range(group)]
    outs = _flash_sweep(
        qs,
        lambda hd, start: k_ref[pl.ds(start, tk), _head_cols(hd)],
        lambda hd, start: v_ref[pl.ds(start, tk), _head_cols(hd)],
        lambda hd, kb: ck_ref[hd, pl.ds(kb, 1), :] * (-LOG2E),
        pl.program_id(2), tq, tk, zscale, HEAD_DIM)
    for hd in range(group):
        o_ref[:, _head_cols(hd)] = outs[hd].astype(o_ref.dtype)


def _fox_attention(h, cum, batch, seq, heads, q_col, k_col, v_col):
    n = h.shape[0]
    tq, tk = min(FA_TQ, seq), min(FA_TK, seq)
    nq, nk = seq // tq, seq // tk
    group = FA_GROUP
    assert heads % group == 0 and q_col % group == 0 and k_col % group == 0 and v_col % group == 0
    gw = group * HEAD_DIM
    cum_k = jnp.transpose(cum[:, :heads].reshape(batch, seq, heads), (0, 2, 1)).reshape(batch, heads, nk, tk)
    kern = functools.partial(_fox_kernel, tq=tq, tk=tk, group=group, zscale=HEAD_DIM ** -0.5 * LOG2E)
    return pl.pallas_call(
        kern,
        out_shape=jax.ShapeDtypeStruct((n, heads * HEAD_DIM), BF16),
        grid=(batch, heads // group, nq),
        in_specs=[pl.BlockSpec((tq, gw), lambda b, g, i: (b * nq + i, q_col // group + g)),
                  pl.BlockSpec((seq, gw), lambda b, g, i: (b, k_col // group + g)),
                  pl.BlockSpec((seq, gw), lambda b, g, i: (b, v_col // group + g)),
                  pl.BlockSpec((None, group, nk, tk), lambda b, g, i: (b, g, 0, 0))],
        out_specs=pl.BlockSpec((tq, gw), lambda b, g, i: (b * nq + i, g)),
        compiler_params=_params(("parallel", "parallel", "arbitrary")),
        name="fox_attention",
    )(h, h, h, cum_k)


def _rope_lanes(t, ra, rb, rc):
    half = ROPE_DIM // 2
    return t * ra + pltpu.roll(t, half, 1) * rb + pltpu.roll(t, LANES - half, 1) * rc


def _mla_kernel(q_ref, kn_ref, kp_ref, v_ref, ra_ref, rb_ref, rc_ref, o_ref, kfull_ref,
                *, tq, tk, group, zscale):
    dq = NOPE_DIM + LANES

    @pl.when(pl.program_id(2) == 0)
    def _():
        for hd in range(group):
            kfull_ref[:, hd * dq:hd * dq + NOPE_DIM] = kn_ref[:, _head_cols(hd)]
            kfull_ref[:, hd * dq + NOPE_DIM:(hd + 1) * dq] = kp_ref[...]

    qs = []
    for hd in range(group):
        q = q_ref[:, hd * dq:(hd + 1) * dq]
        q_pe = _rope_lanes(q[:, NOPE_DIM:].astype(F32), ra_ref[...], rb_ref[...], rc_ref[...])
        qs.append(jnp.concatenate([q[:, :NOPE_DIM], q_pe.astype(BF16)], axis=1))
    outs = _flash_sweep(
        qs,
        lambda hd, start: kfull_ref[pl.ds(start, tk), hd * dq:(hd + 1) * dq],
        lambda hd, start: v_ref[pl.ds(start, tk), _head_cols(hd)],
        None, pl.program_id(2), tq, tk, zscale, V_DIM)
    for hd in range(group):
        o_ref[:, _head_cols(hd)] = outs[hd].astype(o_ref.dtype)


def _mla_attention(q, kv, k_pe, ra, rb, rc, batch, seq, heads):
    n = q.shape[0]
    tq, tk = min(FA_TQ, seq), min(FA_TK, seq)
    nq = seq // tq
    group = FA_GROUP
    assert heads % group == 0
    ng = heads // group
    dq = NOPE_DIM + LANES
    kern = functools.partial(_mla_kernel, tq=tq, tk=tk, group=group,
                             zscale=(NOPE_DIM + ROPE_DIM) ** -0.5 * LOG2E)
    rope_spec = pl.BlockSpec((tq, LANES), lambda b, g, i: (b * nq + i, 0))
    return pl.pallas_call(
        kern,
        out_shape=jax.ShapeDtypeStruct((n, heads * V_DIM), BF16),
        grid=(batch, ng, nq),
        in_specs=[pl.BlockSpec((tq, group * dq), lambda b, g, i: (b * nq + i, g)),
                  pl.BlockSpec((seq, group * NOPE_DIM), lambda b, g, i: (b, g)),
                  pl.BlockSpec((seq, LANES), lambda b, g, i: (b, 0)),
                  pl.BlockSpec((seq, group * V_DIM), lambda b, g, i: (b, ng + g)),
                  rope_spec, rope_spec, rope_spec],
        out_specs=pl.BlockSpec((tq, group * V_DIM), lambda b, g, i: (b * nq + i, g)),
        scratch_shapes=[pltpu.VMEM((seq, group * dq), BF16)],
        compiler_params=_params(("parallel", "parallel", "arbitrary")),
        name="mla_attention",
    )(q, kv, k_pe, kv, ra, rb, rc)


def _mla_prep_kernel(down_ref, pos_ref, qn_ref, kvn_ref, invf_ref,
                     cq_ref, ckv_ref, kpe_ref, ra_ref, rb_ref, rc_ref):
    down = down_ref[...]

    def rms(c, g):
        return c * lax.rsqrt(jnp.mean(c * c, axis=-1, keepdims=True) + RMS_EPS) * g

    cq_ref[...] = rms(down[:, :Q_LORA], qn_ref[...]).astype(cq_ref.dtype)
    ckv_ref[...] = rms(down[:, Q_LORA:Q_LORA + KV_LORA], kvn_ref[...]).astype(ckv_ref.dtype)

    ang = pos_ref[...].astype(F32) * invf_ref[...]
    cos, sin = jnp.cos(ang), jnp.sin(ang)
    lane = lax.broadcasted_iota(jnp.int32, ang.shape, 1)
    half = ROPE_DIM // 2
    ra = jnp.where(lane < ROPE_DIM, cos, 0.0)
    rb = jnp.where((lane >= half) & (lane < ROPE_DIM), sin, 0.0)
    rc = jnp.where(lane < half, -sin, 0.0)
    ra_ref[...] = ra
    rb_ref[...] = rb
    rc_ref[...] = rc
    kpe_ref[...] = _rope_lanes(down[:, Q_LORA + KV_LORA:], ra, rb, rc).astype(kpe_ref.dtype)


def _mla_prep(down, pos, q_norm, kv_norm, inv_freq):
    n, dd = down.shape
    tm = _pick_tile(n, (ROW_TILE, 128))
    row = lambda w: pl.BlockSpec((tm, w), lambda i: (i, 0))
    const = lambda w: pl.BlockSpec((1, w), lambda i: (0, 0))
    return pl.pallas_call(
        _mla_prep_kernel,
        out_shape=(jax.ShapeDtypeStruct((n, Q_LORA), BF16),
                   jax.ShapeDtypeStruct((n, KV_LORA), BF16),
                   jax.ShapeDtypeStruct((n, LANES), BF16),
                   jax.ShapeDtypeStruct((n, LANES), F32),
                   jax.ShapeDtypeStruct((n, LANES), F32),
                   jax.ShapeDtypeStruct((n, LANES), F32)),
        grid=(n // tm,),
        in_specs=[row(dd), row(1), const(Q_LORA), const(KV_LORA), const(LANES)],
        out_specs=(row(Q_LORA), row(KV_LORA), row(LANES), row(LANES), row(LANES), row(LANES)),
        compiler_params=_params(("parallel",)),
        name="mla_prep",
    )(down, pos, q_norm, kv_norm, inv_freq)


def _ln_router_kernel(x_ref, mix_ref, g_ref, b_ref, rw_hi_ref, rw_lo_ref, rb_ref,
                      x1_ref, e_ref, gate_ref, rank_ref, cnt_ref, carry_ref, *, alpha):
    @pl.when(pl.program_id(0) == 0)
    def _():
        carry_ref[...] = jnp.zeros_like(carry_ref)

    x1 = _layer_norm(alpha * x_ref[...] + mix_ref[...], g_ref[...], b_ref[...])
    x1_ref[...] = x1

    a_hi, a_lo = _split2(x1)
    rw_hi, rw_lo = rw_hi_ref[...], rw_lo_ref[...]
    logits = (jnp.dot(a_hi, rw_hi, preferred_element_type=F32)
              + jnp.dot(a_hi, rw_lo, preferred_element_type=F32)
              + jnp.dot(a_lo, rw_hi, preferred_element_type=F32)) + rb_ref[...]

    tm = logits.shape[0]
    lane = lax.broadcasted_iota(jnp.int32, (tm, LANES), 1)
    work = logits
    picks, vals = [], []
    for _ in range(TOP_K):
        mx = jnp.max(work, axis=-1, keepdims=True)
        idx = jnp.min(jnp.where(work == mx, lane, LANES), axis=-1, keepdims=True)
        pick = lane == idx
        picks.append(pick)
        vals.append(mx)
        work = jnp.where(pick, -jnp.inf, work)

    exps = [jnp.exp(v - vals[0]) for v in vals]
    denom = exps[0]
    for e in exps[1:]:
        denom = denom + e

    chosen = jnp.zeros((tm, LANES), F32)
    for pick in picks:
        chosen = chosen + jnp.where(pick, 1.0, 0.0)
    row = lax.broadcasted_iota(jnp.int32, (tm, tm), 0)
    col = lax.broadcasted_iota(jnp.int32, (tm, tm), 1)
    strict_lower = jnp.where(col < row, 1.0, 0.0).astype(BF16)
    before = jnp.dot(strict_lower, chosen.astype(BF16), preferred_element_type=F32) + carry_ref[...]

    e_out = jnp.zeros((tm, LANES), jnp.int32)
    gate_out = jnp.zeros((tm, LANES), F32)
    rank_out = jnp.zeros((tm, LANES), F32)
    for k in range(TOP_K):
        idx = jnp.min(jnp.where(picks[k], lane, LANES), axis=-1, keepdims=True)
        rank = jnp.sum(jnp.where(picks[k], before, 0.0), axis=-1, keepdims=True)
        e_out = jnp.where(lane == k, idx, e_out)
        gate_out = jnp.where(lane == k, exps[k] / denom, gate_out)
        rank_out = jnp.where(lane == k, rank, rank_out)
    e_ref[...] = e_out
    gate_ref[...] = gate_out
    rank_ref[...] = rank_out.astype(jnp.int32)

    carry_ref[...] = carry_ref[...] + jnp.sum(chosen, axis=0, keepdims=True)
    cnt_ref[...] = carry_ref[...]


def _ln_router(x, mix, g, b, rw_hi, rw_lo, rb, alpha):
    n, d = x.shape
    tm = _pick_tile(n, (ROW_TILE, 128))
    row = lambda w: pl.BlockSpec((tm, w), lambda i: (i, 0))
    const = lambda r, w: pl.BlockSpec((r, w), lambda i: (0, 0))
    return pl.pallas_call(
        functools.partial(_ln_router_kernel, alpha=alpha),
        out_shape=(jax.ShapeDtypeStruct((n, d), F32),
                   jax.ShapeDtypeStruct((n, LANES), jnp.int32),
                   jax.ShapeDtypeStruct((n, LANES), F32),
                   jax.ShapeDtypeStruct((n, LANES), jnp.int32),
                   jax.ShapeDtypeStruct((1, LANES), F32)),
        grid=(n // tm,),
        in_specs=[row(d), row(d), const(1, d), const(1, d), const(d, LANES), const(d, LANES),
                  const(1, LANES)],
        out_specs=(row(d), row(LANES), row(LANES), row(LANES), const(1, LANES)),
        scratch_shapes=[pltpu.VMEM((1, LANES), F32)],
        compiler_params=_params(("arbitrary",)),
        name="ln_router",
    )(x, mix, g, b, rw_hi, rw_lo, rb)


def _expert_kernel(blk_e_ref, n_used_ref, row_tok_ref,
                   x_hbm, wgu_ref, bgu_ref, wdn_ref, bdn_ref, y_ref, xbuf, sem, *, bm, d_expert):
    i = pl.program_id(0)
    n_used = n_used_ref[0]

    def gather_copy(tok, r, slot):
        return pltpu.make_async_copy(x_hbm.at[pl.ds(tok, 1)], xbuf.at[slot, pl.ds(r, 1)], sem.at[slot])

    def issue(blk, slot):
        def body(r, _):
            gather_copy(row_tok_ref[blk * bm + r], r, slot).start()
            return 0
        lax.fori_loop(0, bm, body, 0, unroll=8)

    @pl.when(jnp.logical_and(i == 0, n_used > 0))
    def _():
        issue(0, 0)

    @pl.when(i + 1 < n_used)
    def _():
        issue(i + 1, (i + 1) % 2)

    @pl.when(i < n_used)
    def _():
        slot = i % 2

        def wait_body(r, _):
            gather_copy(0, r, slot).wait()
            return 0
        lax.fori_loop(0, bm, wait_body, 0, unroll=8)

        xb = xbuf[slot].astype(BF16)
        hb = jnp.dot(xb, wgu_ref[...], preferred_element_type=F32) + bgu_ref[...]
        g = jnp.minimum(hb[:, :d_expert], SWIGLU_LIMIT)
        u = jnp.clip(hb[:, d_expert:], -SWIGLU_LIMIT, SWIGLU_LIMIT)
        act = (u + 1.0) * (g * (1.0 / (1.0 + jnp.exp(-SWIGLU_ALPHA * g))))
        y_ref[...] = jnp.dot(act.astype(BF16), wdn_ref[...], preferred_element_type=F32) + bdn_ref[...]

    @pl.when(i >= n_used)
    def _():
        y_ref[...] = jnp.zeros_like(y_ref)


def _experts(x1, blk_e, n_used, row_tok, w_gu, b_gu, w_dn, b_dn, bm):
    n, d = x1.shape
    n_exp, _, f2 = w_gu.shape
    d_expert = f2 // 2
    n_rows = row_tok.shape[0]
    n_blk = n_rows // bm
    grid_spec = pltpu.PrefetchScalarGridSpec(
        num_scalar_prefetch=3,
        grid=(n_blk,),
        in_specs=[pl.BlockSpec(memory_space=pl.ANY),
                  pl.BlockSpec((None, d, f2), lambda i, be, nu, rt: (be[i], 0, 0)),
                  pl.BlockSpec((None, 1, f2), lambda i, be, nu, rt: (be[i], 0, 0)),
                  pl.BlockSpec((None, d_expert, d), lambda i, be, nu, rt: (be[i], 0, 0)),
                  pl.BlockSpec((None, 1, d), lambda i, be, nu, rt: (be[i], 0, 0))],
        out_specs=pl.BlockSpec((bm, d), lambda i, be, nu, rt: (i, 0)),
        scratch_shapes=[pltpu.VMEM((2, bm, d), F32), pltpu.SemaphoreType.DMA((2,))],
    )
    return pl.pallas_call(
        functools.partial(_expert_kernel, bm=bm, d_expert=d_expert),
        out_shape=jax.ShapeDtypeStruct((n_rows, d), F32),
        grid_spec=grid_spec,
        compiler_params=_params(("arbitrary",)),
        name="experts",
    )(blk_e, n_used, row_tok, x1, w_gu, b_gu.reshape(n_exp, 1, f2), w_dn, b_dn.reshape(n_exp, 1, d))


def _combine_kernel(pos_ref, y_hbm, x1_ref, gate_ref, g_ref, b_ref, x2_ref, xb_ref, ybuf, sem, *, tm, alpha):
    i = pl.program_id(0)
    n_steps = pl.num_programs(0)

    def gather_copy(src_row, k, t, slot):
        return pltpu.make_async_copy(y_hbm.at[pl.ds(src_row, 1)], ybuf.at[slot, k, pl.ds(t, 1)], sem.at[slot])

    def issue(step, slot):
        def body(t, _):
            for k in range(TOP_K):
                gather_copy(pos_ref[(step * tm + t) * TOP_K + k], k, t, slot).start()
            return 0
        lax.fori_loop(0, tm, body, 0, unroll=2)

    @pl.when(i == 0)
    def _():
        issue(0, 0)

    @pl.when(i + 1 < n_steps)
    def _():
        issue(i + 1, (i + 1) % 2)

    slot = i % 2

    def wait_body(t, _):
        for k in range(TOP_K):
            gather_copy(0, k, t, slot).wait()
        return 0
    lax.fori_loop(0, tm, wait_body, 0, unroll=2)

    gate = gate_ref[...]
    ffn = gate[:, 0:1] * ybuf[slot, 0]
    for k in range(1, TOP_K):
        ffn = ffn + gate[:, k:k + 1] * ybuf[slot, k]
    x2 = _layer_norm(alpha * x1_ref[...] + ffn, g_ref[...], b_ref[...])
    x2_ref[...] = x2
    xb_ref[...] = x2.astype(BF16)


def _combine_ln(pos_flat, y, x1, gate, g, b, alpha):
    n, d = x1.shape
    tm = _pick_tile(n, (COMBINE_TM,))
    grid_spec = pltpu.PrefetchScalarGridSpec(
        num_scalar_prefetch=1,
        grid=(n // tm,),
        in_specs=[pl.BlockSpec(memory_space=pl.ANY),
                  pl.BlockSpec((tm, d), lambda i, p: (i, 0)),
                  pl.BlockSpec((tm, LANES), lambda i, p: (i, 0)),
                  pl.BlockSpec((1, d), lambda i, p: (0, 0)),
                  pl.BlockSpec((1, d), lambda i, p: (0, 0))],
        out_specs=(pl.BlockSpec((tm, d), lambda i, p: (i, 0)),
                   pl.BlockSpec((tm, d), lambda i, p: (i, 0))),
        scratch_shapes=[pltpu.VMEM((2, TOP_K, tm, d), F32), pltpu.SemaphoreType.DMA((2,))],
    )
    return pl.pallas_call(
        functools.partial(_combine_kernel, tm=tm, alpha=alpha),
        out_shape=(jax.ShapeDtypeStruct((n, d), F32), jax.ShapeDtypeStruct((n, d), BF16)),
        grid_spec=grid_spec,
        compiler_params=_params(("arbitrary",)),
        name="combine_ln",
    )(pos_flat, y, x1, gate, g, b)


def _pad_cols(w, width):
    return jnp.pad(w, ((0, 0), (0, width - w.shape[1])))


def _moe_layer(x, mix, ln_g, ln_b, router_w, router_b, w_gu, b_gu, w_dn, b_dn, ln2_g, ln2_b, alpha):
    n, d = x.shape
    n_exp = router_w.shape[1]
    bm = EXPERT_BM

    rw = _pad_cols(router_w, LANES)
    rw_hi = rw.astype(BF16)
    rw_lo = (rw - rw_hi.astype(F32)).astype(BF16)
    rb = jnp.full((1, LANES), MASK_VALUE, F32).at[0, :n_exp].set(router_b)
    x1, top_e, gate, rank, cnt = _ln_router(x, mix, ln_g[None], ln_b[None], rw_hi, rw_lo, rb, alpha)

    counts = cnt[0, :n_exp].astype(jnp.int32)
    padded = (counts + bm - 1) // bm * bm
    pend = jnp.cumsum(padded)
    pstart = pend - padded
    top_e = top_e[:, :TOP_K]
    pos = (pstart[top_e] + rank[:, :TOP_K]).reshape(-1)
    n_rows = -(-(n * TOP_K + n_exp * (bm - 1)) // bm) * bm
    n_blk = n_rows // bm
    tok = jnp.repeat(jnp.arange(n, dtype=jnp.int32), TOP_K)
    row_tok = jnp.zeros((n_rows,), jnp.int32).at[pos].set(tok)
    blk_e = jnp.minimum(jnp.searchsorted(pend, jnp.arange(n_blk, dtype=jnp.int32) * bm, side='right'),
                        n_exp - 1).astype(jnp.int32)
    n_used = (pend[-1:] // bm).astype(jnp.int32)

    y = _experts(x1, blk_e, n_used, row_tok, w_gu.astype(BF16), b_gu, w_dn.astype(BF16), b_dn, bm)
    return _combine_ln(pos, y, x1, gate, ln2_g[None], ln2_b[None], alpha)


def _even_mixer(xb, w_in, b_f, w_o, batch, seq):
    sbw = SB_HEADS * HEAD_DIM
    fxw = FOX_HEADS * HEAD_DIM
    main = 3 * sbw + 3 * fxw
    h = _matmul(xb, w_in[:, :main].astype(BF16), BF16)
    w_f = _pad_cols(w_in[:, main:], LANES).astype(BF16)
    cum = _fox_gate(xb, w_f, _pad_cols(b_f[None], LANES), batch, seq)
    o_sb = _sb_attention(h, batch, seq, SB_HEADS, 0, SB_HEADS, 2 * SB_HEADS)
    c0 = 3 * SB_HEADS
    o_fx = _fox_attention(h, cum, batch, seq, FOX_HEADS, c0, c0 + FOX_HEADS, c0 + 2 * FOX_HEADS)
    o = jnp.concatenate([o_sb, o_fx], axis=-1)
    return _matmul(o, w_o.astype(BF16), F32)


def _mla_mixer(xb, pos, w_down, q_norm, kv_norm, w_uq, w_ukv, w_o, batch, seq):
    d = xb.shape[1]
    hds = MLA_HEADS
    w_down_p = _pad_cols(w_down, Q_LORA + KV_LORA + LANES).astype(BF16)
    down = _matmul(xb, w_down_p, F32)
    half = ROPE_DIM // 2
    freq = ROPE_THETA ** (-jnp.arange(0, ROPE_DIM, 2, dtype=F32) / ROPE_DIM)
    inv_freq = jnp.concatenate([freq, freq, jnp.zeros((LANES - ROPE_DIM,), F32)])[None]
    cq, ckv, k_pe, ra, rb, rc = _mla_prep(down, pos, q_norm[None], kv_norm[None], inv_freq)

    wq = w_uq.reshape(Q_LORA, hds, NOPE_DIM + ROPE_DIM)
    wq = jnp.pad(wq, ((0, 0), (0, 0), (0, LANES - ROPE_DIM))).reshape(Q_LORA, hds * (NOPE_DIM + LANES))
    wkv = w_ukv.reshape(KV_LORA, hds, NOPE_DIM + V_DIM)
    wkv = jnp.concatenate([wkv[:, :, :NOPE_DIM].reshape(KV_LORA, hds * NOPE_DIM),
                           wkv[:, :, NOPE_DIM:].reshape(KV_LORA, hds * V_DIM)], axis=1)
    q = _matmul(cq, wq.astype(BF16), BF16)
    kv = _matmul(ckv, wkv.astype(BF16), BF16)
    o = _mla_attention(q, kv, k_pe, ra, rb, rc, batch, seq, hds)
    return _matmul(o, w_o.astype(BF16), F32)


def kernel(x, positions, ln_mix_g, ln_mix_b, ln_ffn_g, ln_ffn_b, even_w_in, fox_b_f, even_w_o, mla_w_down, mla_q_norm, mla_kv_norm, mla_w_uq, mla_w_ukv, mla_w_o, router_w, router_b, expert_w_gate_up, expert_b_gate_up, expert_w_down, expert_b_down):
    batch, seq, d = x.shape
    depth = ln_mix_g.shape[0]
    alpha = (2 * depth) ** 0.25
    n = batch * seq
    xf = x.reshape(n, d)
    xb = xf.astype(BF16)
    pos = positions.reshape(n, 1)
    for layer in range(depth):
        i = layer // 2
        if layer % 2 == 0:
            mix = _even_mixer(xb, even_w_in[i], fox_b_f[i], even_w_o[i], batch, seq)
        else:
            mix = _mla_mixer(xb, pos, mla_w_down[i], mla_q_norm[i], mla_kv_norm[i],
                             mla_w_uq[i], mla_w_ukv[i], mla_w_o[i], batch, seq)
        xf, xb = _moe_layer(xf, mix, ln_mix_g[layer], ln_mix_b[layer], router_w[layer], router_b[layer],
                            expert_w_gate_up[layer], expert_b_gate_up[layer],
                            expert_w_down[layer], expert_b_down[layer],
                            ln_ffn_g[layer], ln_ffn_b[layer], alpha)
    return xf.reshape(batch, seq, d)
```

```python
import functools

import jax
import jax.numpy as jnp
from jax import lax
from jax.experimental import pallas as pl
from jax.experimental.pallas import tpu as pltpu

F32 = jnp.float32
BF16 = jnp.bfloat16

HEAD_DIM = 128
SB_HEADS = 8
FOX_HEADS = 8
MLA_HEADS = 16
Q_LORA = 512
KV_LORA = 512
NOPE_DIM = 128
ROPE_DIM = 64
V_DIM = 128
ROPE_THETA = 10000.0
N_EXPERTS = 32
TOP_K = 4
SWIGLU_LIMIT = 7.0
SWIGLU_ALPHA = 1.702
LN_EPS = 1e-5
RMS_EPS = 1e-6

LANES = 128
MASK_VALUE = -1e30
VMEM_LIMIT = 56 * 1024 * 1024
LOG2E = 1.4426950408889634
LN2 = 0.6931471805599453

MM_TM = 1024
MM_TN = 512
SB_TQ = 512
SB_CK = 256
SB_GROUP = 4
FA_TILE = 512
FA_GROUP = 4
ROW_TILE = 256
GATE_TILE = 512
EXPERT_BM = 512
COMBINE_TM = 128


def _params(sem):
    return pltpu.CompilerParams(dimension_semantics=sem, vmem_limit_bytes=VMEM_LIMIT)


def _pick_tile(n, prefs):
    for t in prefs:
        if n % t == 0:
            return t
    return n


def _mm_kernel(a_ref, w_ref, o_ref):
    o_ref[...] = jnp.dot(a_ref[...].astype(BF16), w_ref[...],
                         preferred_element_type=F32).astype(o_ref.dtype)


def _matmul(a, w, out_dtype):
    m, k = a.shape
    n = w.shape[1]
    tm = _pick_tile(m, (MM_TM, 512, 256, 128))
    tn = _pick_tile(n, (MM_TN, 384, 256, 128))
    return pl.pallas_call(
        _mm_kernel,
        out_shape=jax.ShapeDtypeStruct((m, n), out_dtype),
        grid=(m // tm, n // tn),
        in_specs=[pl.BlockSpec((tm, k), lambda i, j: (i, 0)),
                  pl.BlockSpec((k, tn), lambda i, j: (0, j))],
        out_specs=pl.BlockSpec((tm, tn), lambda i, j: (i, j)),
        compiler_params=_params(("parallel", "parallel")),
        name="matmul",
    )(a, w)


def _split2(x):
    hi = x.astype(BF16)
    lo = (x - hi.astype(F32)).astype(BF16)
    return hi, lo


def _split3(x):
    h1 = x.astype(BF16)
    r1 = x - h1.astype(F32)
    h2 = r1.astype(BF16)
    h3 = (r1 - h2.astype(F32)).astype(BF16)
    return h1, h2, h3


def _log1pexp_neg_abs(z):
    return jnp.log(1.0 + jnp.exp(-jnp.abs(z)))


def _neg_abs(x):
    bits = lax.bitcast_convert_type(x, jnp.uint32) | jnp.uint32(0x80000000)
    return lax.bitcast_convert_type(bits, F32)


def _store_packed_rows(xp_ref, x):
    half = x.shape[1] // 2
    lo = lax.bitcast_convert_type(x[:, :half].astype(BF16).astype(F32), jnp.uint32)
    hi = lax.bitcast_convert_type(x[:, half:].astype(BF16).astype(F32), jnp.uint32)
    xp_ref[...] = (lo >> 16) | (hi & jnp.uint32(0xFFFF0000))


def _unpack_rows(w):
    lo = lax.bitcast_convert_type(w << 16, F32).astype(BF16)
    hi = lax.bitcast_convert_type(w & jnp.uint32(0xFFFF0000), F32).astype(BF16)
    return jnp.concatenate([lo, hi], axis=1)


def _layer_norm(h, g, b):
    mu = jnp.mean(h, axis=-1, keepdims=True)
    c = h - mu
    var = jnp.mean(c * c, axis=-1, keepdims=True)
    return c * lax.rsqrt(var + LN_EPS) * g + b


def _head_cols(hd):
    return slice(hd * HEAD_DIM, (hd + 1) * HEAD_DIM)


def _fox_gate_kernel(x_ref, w_ref, b_ref, o_ref, carry_ref):
    @pl.when(pl.program_id(1) == 0)
    def _():
        carry_ref[...] = jnp.zeros_like(carry_ref)

    f = jnp.dot(x_ref[...], w_ref[...], preferred_element_type=F32) + b_ref[...]
    log_f = jnp.minimum(f, 0.0) - _log1pexp_neg_abs(f)
    ts = log_f.shape[0]
    row = lax.broadcasted_iota(jnp.int32, (ts, ts), 0)
    col = lax.broadcasted_iota(jnp.int32, (ts, ts), 1)
    tri = jnp.where(col <= row, 1.0, 0.0).astype(BF16)
    cum = carry_ref[...]
    for part in _split3(log_f):
        cum = cum + jnp.dot(tri, part, preferred_element_type=F32)
    o_ref[...] = cum
    carry_ref[...] = cum[ts - 1:ts, :]


def _fox_gate(xb, w_f, b_f, batch, seq):
    n, d = xb.shape
    ts = _pick_tile(seq, (GATE_TILE, 256, 128))
    ns = seq // ts
    return pl.pallas_call(
        _fox_gate_kernel,
        out_shape=jax.ShapeDtypeStruct((n, LANES), F32),
        grid=(batch, ns),
        in_specs=[pl.BlockSpec((ts, d), lambda b, j: (b * ns + j, 0)),
                  pl.BlockSpec((d, LANES), lambda b, j: (0, 0)),
                  pl.BlockSpec((1, LANES), lambda b, j: (0, 0))],
        out_specs=pl.BlockSpec((ts, LANES), lambda b, j: (b * ns + j, 0)),
        scratch_shapes=[pltpu.VMEM((1, LANES), F32)],
        compiler_params=_params(("parallel", "arbitrary")),
        name="fox_gate",
    )(xb, w_f, b_f)


def _sb_kernel(q_ref, k_ref, v_ref, u_ref, o_ref, *, tq, ck, group):
    qi = pl.program_id(2)
    n_diag = tq // ck
    n_below = qi * n_diag

    def evaluate(units):
        n = len(units)
        starts = [pl.multiple_of(kb * ck, ck) for _, kb, _, _ in units]
        zs, sps, sums, stricts, laters = [None] * n, [None] * n, [None] * n, [None] * n, [None] * n
        pvs, tots = [None] * n, [None] * n

        def scores(u):
            hd, _, r0, masked = units[u]
            z = lax.dot_general(q_ref[r0:, _head_cols(hd)], k_ref[pl.ds(starts[u], ck), _head_cols(hd)],
                                (((1,), (1,)), ((), ())), preferred_element_type=F32)
            sp = jnp.maximum(z, 0.0) + jnp.log(1.0 + jnp.exp2(_neg_abs(z))) * (1.0 / LN2)
            zs[u], sps[u], sums[u] = z, sp, sp
            if masked:
                q_pos = qi * tq + r0 + lax.broadcasted_iota(jnp.int32, z.shape, 0)
                stricts[u] = (starts[u] + lax.broadcasted_iota(jnp.int32, z.shape, 1)) < q_pos
                sums[u] = jnp.where(stricts[u], sp, 0.0)

        def prefix(u):
            laters[u] = jnp.dot(sums[u].astype(BF16), u_ref[...], preferred_element_type=F32)

        def values(u):
            hd = units[u][0]
            w = jnp.exp2(zs[u] - sps[u] - laters[u])
            if stricts[u] is not None:
                w = jnp.where(stricts[u], w, 0.0)
            pvs[u] = jnp.dot(w.astype(BF16), v_ref[pl.ds(starts[u], ck), _head_cols(hd)],
                             preferred_element_type=F32)
            tots[u] = laters[u][:, :1] + sums[u][:, :1]

        for t in range(n + 2):
            if t >= 2:
                values(t - 2)
            if 1 <= t <= n:
                prefix(t - 1)
            if t < n:
                scores(t)
        return pvs, tots

    def fold(acc, later, pv, tot, r0):
        acc_s = acc[r0:] + pv * jnp.exp2(-later[r0:])
        later_s = later[r0:] + tot
        if r0:
            return (jnp.concatenate([acc[:r0], acc_s], axis=0),
                    jnp.concatenate([later[:r0], later_s], axis=0))
        return acc_s, later_s

    diag_units = [(hd, n_below + d, d * ck, True) for hd in range(group) for d in range(n_diag - 1, -1, -1)]
    pvs, tots = evaluate(diag_units)
    accs, laters = [], []
    for hd in range(group):
        acc, later = jnp.zeros((tq, HEAD_DIM), F32), jnp.zeros((tq, 1), F32)
        for u in range(hd * n_diag, (hd + 1) * n_diag):
            acc, later = fold(acc, later, pvs[u], tots[u], diag_units[u][2])
        accs.append(acc)
        laters.append(later)

    def body(i, carry):
        accs, laters = carry
        kb = n_below - 1 - 2 * i
        pvs, tots = evaluate([(hd, kb - c, 0, False) for hd in range(group) for c in range(2)])
        out = [fold(*fold(accs[hd], laters[hd], pvs[2 * hd], tots[2 * hd], 0),
                    pvs[2 * hd + 1], tots[2 * hd + 1], 0) for hd in range(group)]
        return tuple(o[0] for o in out), tuple(o[1] for o in out)

    accs, _ = lax.fori_loop(0, n_below // 2, body, (tuple(accs), tuple(laters)))
    for hd in range(group):
        o_ref[:, _head_cols(hd)] = accs[hd].astype(o_ref.dtype)


def _sb_attention(h, batch, seq, heads, q_col, k_col, v_col):
    n = h.shape[0]
    tq, ck = min(SB_TQ, seq), min(SB_CK, seq // 2)
    nq = seq // tq
    group = SB_GROUP
    assert (tq // ck) % 2 == 0 and heads % group == 0
    assert q_col % group == 0 and k_col % group == 0 and v_col % group == 0
    gw = group * HEAD_DIM
    j = lax.broadcasted_iota(jnp.int32, (ck, ck), 0)
    s = lax.broadcasted_iota(jnp.int32, (ck, ck), 1)
    u = jnp.where(j > s, 1.0, 0.0).astype(BF16)
    kern = functools.partial(_sb_kernel, tq=tq, ck=ck, group=group)
    return pl.pallas_call(
        kern,
        out_shape=jax.ShapeDtypeStruct((n, heads * HEAD_DIM), BF16),
        grid=(batch, heads // group, nq),
        in_specs=[pl.BlockSpec((tq, gw), lambda b, g, i: (b * nq + i, q_col // group + g)),
                  pl.BlockSpec((seq, gw), lambda b, g, i: (b, k_col // group + g)),
                  pl.BlockSpec((seq, gw), lambda b, g, i: (b, v_col // group + g)),
                  pl.BlockSpec((ck, ck), lambda b, g, i: (0, 0))],
        out_specs=pl.BlockSpec((tq, gw), lambda b, g, i: (b * nq + i, g)),
        compiler_params=_params(("parallel", "parallel", "arbitrary")),
        name="sb_attention",
    )(h, h, h, u)


def _flash_sweep(qs, k_fn, v_fn, bias_fn, qi, tile, dv):
    n = len(qs)

    def sweep_step(kb, states, masked):
        start = pl.multiple_of(kb * tile, tile)
        scores, new = [None] * n, [None] * n

        def score(hd):
            s = lax.dot_general(qs[hd], k_fn(hd, start), (((1,), (1,)), ((), ())), preferred_element_type=F32)
            if bias_fn is not None:
                s = s + bias_fn(hd, kb)
            if masked:
                s = jnp.where(lax.broadcasted_iota(jnp.int32, s.shape, 1)
                              <= lax.broadcasted_iota(jnp.int32, s.shape, 0), s, MASK_VALUE)
            scores[hd] = s

        def update(hd):
            m, l, acc = states[hd]
            s = scores[hd]
            m_new = jnp.maximum(m, jnp.max(s, axis=-1, keepdims=True))
            p = jnp.exp2(s - m_new)
            corr = jnp.exp2(m - m_new)
            l = corr * l + jnp.sum(p, axis=-1, keepdims=True)
            acc = corr * acc + jnp.dot(p.astype(BF16), v_fn(hd, start), preferred_element_type=F32)
            new[hd] = (m_new, l, acc)

        for i in range(n + 1):
            if i < n:
                score(i)
            if i >= 1:
                update(i - 1)
        return tuple(new)

    init = (jnp.full((tile, 1), MASK_VALUE, F32), jnp.zeros((tile, 1), F32), jnp.zeros((tile, dv), F32))
    states = lax.fori_loop(0, qi, lambda kb, st: sweep_step(kb, st, False), tuple(init for _ in range(n)))
    states = sweep_step(qi, states, True)
    return [acc / l for _, l, acc in states]


def _fox_kernel(q_ref, k_ref, v_ref, ck_ref, o_ref, *, tile, group):
    qs = [q_ref[:, _head_cols(hd)] for hd in range(group)]
    outs = _flash_sweep(
        qs,
        lambda hd, start: k_ref[pl.ds(start, tile), _head_cols(hd)],
        lambda hd, start: v_ref[pl.ds(start, tile), _head_cols(hd)],
        lambda hd, kb: ck_ref[hd, pl.ds(kb, 1), :] * (-LOG2E),
        pl.program_id(2), tile, HEAD_DIM)
    for hd in range(group):
        o_ref[:, _head_cols(hd)] = outs[hd].astype(o_ref.dtype)


def _fox_attention(h, cum, batch, seq, heads, q_col, k_col, v_col):
    n = h.shape[0]
    tile = min(FA_TILE, seq)
    nq = seq // tile
    group = FA_GROUP
    assert heads % group == 0 and q_col % group == 0 and k_col % group == 0 and v_col % group == 0
    gw = group * HEAD_DIM
    cum_k = jnp.transpose(cum[:, :heads].reshape(batch, seq, heads), (0, 2, 1)).reshape(batch, heads, nq, tile)
    kern = functools.partial(_fox_kernel, tile=tile, group=group)
    return pl.pallas_call(
        kern,
        out_shape=jax.ShapeDtypeStruct((n, heads * HEAD_DIM), BF16),
        grid=(batch, heads // group, nq),
        in_specs=[pl.BlockSpec((tile, gw), lambda b, g, i: (b * nq + i, q_col // group + g)),
                  pl.BlockSpec((seq, gw), lambda b, g, i: (b, k_col // group + g)),
                  pl.BlockSpec((seq, gw), lambda b, g, i: (b, v_col // group + g)),
                  pl.BlockSpec((None, group, nq, tile), lambda b, g, i: (b, g, 0, 0))],
        out_specs=pl.BlockSpec((tile, gw), lambda b, g, i: (b * nq + i, g)),
        compiler_params=_params(("parallel", "parallel", "arbitrary")),
        name="fox_attention",
    )(h, h, h, cum_k)


def _rope_lanes(t, ra, rb, rc):
    half = ROPE_DIM // 2
    return t * ra + pltpu.roll(t, half, 1) * rb + pltpu.roll(t, LANES - half, 1) * rc


def _mla_kernel(q_ref, kn_ref, kp_ref, v_ref, ra_ref, rb_ref, rc_ref, o_ref, kfull_ref,
                *, tile, group):
    dq = NOPE_DIM + LANES

    @pl.when(pl.program_id(2) == 0)
    def _():
        for hd in range(group):
            kfull_ref[:, hd * dq:hd * dq + NOPE_DIM] = kn_ref[:, _head_cols(hd)]
            kfull_ref[:, hd * dq + NOPE_DIM:(hd + 1) * dq] = kp_ref[...]

    qs = []
    for hd in range(group):
        q = q_ref[:, hd * dq:(hd + 1) * dq]
        q_pe = _rope_lanes(q[:, NOPE_DIM:].astype(F32), ra_ref[...], rb_ref[...], rc_ref[...])
        qs.append(jnp.concatenate([q[:, :NOPE_DIM], q_pe.astype(BF16)], axis=1))
    outs = _flash_sweep(
        qs,
        lambda hd, start: kfull_ref[pl.ds(start, tile), hd * dq:(hd + 1) * dq],
        lambda hd, start: v_ref[pl.ds(start, tile), _head_cols(hd)],
        None, pl.program_id(2), tile, V_DIM)
    for hd in range(group):
        o_ref[:, _head_cols(hd)] = outs[hd].astype(o_ref.dtype)


def _mla_attention(q, kv, k_pe, ra, rb, rc, batch, seq, heads):
    n = q.shape[0]
    tile = min(FA_TILE, seq)
    nq = seq // tile
    group = FA_GROUP
    assert heads % group == 0
    ng = heads // group
    dq = NOPE_DIM + LANES
    kern = functools.partial(_mla_kernel, tile=tile, group=group)
    rope_spec = pl.BlockSpec((tile, LANES), lambda b, g, i: (b * nq + i, 0))
    return pl.pallas_call(
        kern,
        out_shape=jax.ShapeDtypeStruct((n, heads * V_DIM), BF16),
        grid=(batch, ng, nq),
        in_specs=[pl.BlockSpec((tile, group * dq), lambda b, g, i: (b * nq + i, g)),
                  pl.BlockSpec((seq, group * NOPE_DIM), lambda b, g, i: (b, g)),
                  pl.BlockSpec((seq, LANES), lambda b, g, i: (b, 0)),
                  pl.BlockSpec((seq, group * V_DIM), lambda b, g, i: (b, ng + g)),
                  rope_spec, rope_spec, rope_spec],
        out_specs=pl.BlockSpec((tile, group * V_DIM), lambda b, g, i: (b * nq + i, g)),
        scratch_shapes=[pltpu.VMEM((seq, group * dq), BF16)],
        compiler_params=_params(("parallel", "parallel", "arbitrary")),
        name="mla_attention",
    )(q, kv, k_pe, kv, ra, rb, rc)


def _mla_prep_kernel(down_ref, pos_ref, qn_ref, kvn_ref, invf_ref,
                     cq_ref, ckv_ref, kpe_ref, ra_ref, rb_ref, rc_ref):
    down = down_ref[...]

    def rms(c, g):
        return c * lax.rsqrt(jnp.mean(c * c, axis=-1, keepdims=True) + RMS_EPS) * g

    cq_ref[...] = rms(down[:, :Q_LORA], qn_ref[...]).astype(cq_ref.dtype)
    ckv_ref[...] = rms(down[:, Q_LORA:Q_LORA + KV_LORA], kvn_ref[...]).astype(ckv_ref.dtype)

    ang = pos_ref[...].astype(F32) * invf_ref[...]
    cos, sin = jnp.cos(ang), jnp.sin(ang)
    lane = lax.broadcasted_iota(jnp.int32, ang.shape, 1)
    half = ROPE_DIM // 2
    ra = jnp.where(lane < ROPE_DIM, cos, 0.0)
    rb = jnp.where((lane >= half) & (lane < ROPE_DIM), sin, 0.0)
    rc = jnp.where(lane < half, -sin, 0.0)
    ra_ref[...] = ra
    rb_ref[...] = rb
    rc_ref[...] = rc
    kpe_ref[...] = _rope_lanes(down[:, Q_LORA + KV_LORA:], ra, rb, rc).astype(kpe_ref.dtype)


def _mla_prep(down, pos, q_norm, kv_norm, inv_freq):
    n, dd = down.shape
    tm = _pick_tile(n, (ROW_TILE, 128))
    row = lambda w: pl.BlockSpec((tm, w), lambda i: (i, 0))
    const = lambda w: pl.BlockSpec((1, w), lambda i: (0, 0))
    return pl.pallas_call(
        _mla_prep_kernel,
        out_shape=(jax.ShapeDtypeStruct((n, Q_LORA), BF16),
                   jax.ShapeDtypeStruct((n, KV_LORA), BF16),
                   jax.ShapeDtypeStruct((n, LANES), BF16),
                   jax.ShapeDtypeStruct((n, LANES), F32),
                   jax.ShapeDtypeStruct((n, LANES), F32),
                   jax.ShapeDtypeStruct((n, LANES), F32)),
        grid=(n // tm,),
        in_specs=[row(dd), row(1), const(Q_LORA), const(KV_LORA), const(LANES)],
        out_specs=(row(Q_LORA), row(KV_LORA), row(LANES), row(LANES), row(LANES), row(LANES)),
        compiler_params=_params(("parallel",)),
        name="mla_prep",
    )(down, pos, q_norm, kv_norm, inv_freq)


def _ln_router_kernel(x_ref, mix_ref, g_ref, b_ref, rw_hi_ref, rw_lo_ref, rb_ref,
                      x1_ref, xp_ref, e_ref, gate_ref, rank_ref, cnt_ref, carry_ref, *, alpha):
    @pl.when(pl.program_id(0) == 0)
    def _():
        carry_ref[...] = jnp.zeros_like(carry_ref)

    x1 = _layer_norm(alpha * x_ref[...] + mix_ref[...], g_ref[...], b_ref[...])
    x1_ref[...] = x1
    _store_packed_rows(xp_ref, x1)

    a_hi, a_lo = _split2(x1)
    rw_hi, rw_lo = rw_hi_ref[...], rw_lo_ref[...]
    logits = (jnp.dot(a_hi, rw_hi, preferred_element_type=F32)
              + jnp.dot(a_hi, rw_lo, preferred_element_type=F32)
              + jnp.dot(a_lo, rw_hi, preferred_element_type=F32)) + rb_ref[...]

    tm = logits.shape[0]
    lane = lax.broadcasted_iota(jnp.int32, (tm, LANES), 1)
    work = logits
    picks, vals = [], []
    for _ in range(TOP_K):
        mx = jnp.max(work, axis=-1, keepdims=True)
        idx = jnp.min(jnp.where(work == mx, lane, LANES), axis=-1, keepdims=True)
        pick = lane == idx
        picks.append(pick)
        vals.append(mx)
        work = jnp.where(pick, -jnp.inf, work)

    exps = [jnp.exp(v - vals[0]) for v in vals]
    denom = exps[0]
    for e in exps[1:]:
        denom = denom + e

    chosen = jnp.zeros((tm, LANES), F32)
    for pick in picks:
        chosen = chosen + jnp.where(pick, 1.0, 0.0)
    row = lax.broadcasted_iota(jnp.int32, (tm, tm), 0)
    col = lax.broadcasted_iota(jnp.int32, (tm, tm), 1)
    strict_lower = jnp.where(col < row, 1.0, 0.0).astype(BF16)
    before = jnp.dot(strict_lower, chosen.astype(BF16), preferred_element_type=F32) + carry_ref[...]

    e_out = jnp.zeros((tm, LANES), jnp.int32)
    gate_out = jnp.zeros((tm, LANES), F32)
    rank_out = jnp.zeros((tm, LANES), F32)
    for k in range(TOP_K):
        idx = jnp.min(jnp.where(picks[k], lane, LANES), axis=-1, keepdims=True)
        rank = jnp.sum(jnp.where(picks[k], before, 0.0), axis=-1, keepdims=True)
        e_out = jnp.where(lane == k, idx, e_out)
        gate_out = jnp.where(lane == k, exps[k] / denom, gate_out)
        rank_out = jnp.where(lane == k, rank, rank_out)
    e_ref[...] = e_out
    gate_ref[...] = gate_out
    rank_ref[...] = rank_out.astype(jnp.int32)

    carry_ref[...] = carry_ref[...] + jnp.sum(chosen, axis=0, keepdims=True)
    cnt_ref[...] = carry_ref[...]


def _ln_router(x, mix, g, b, rw_hi, rw_lo, rb, alpha):
    n, d = x.shape
    tm = _pick_tile(n, (ROW_TILE, 128))
    row = lambda w: pl.BlockSpec((tm, w), lambda i: (i, 0))
    const = lambda r, w: pl.BlockSpec((r, w), lambda i: (0, 0))
    return pl.pallas_call(
        functools.partial(_ln_router_kernel, alpha=alpha),
        out_shape=(jax.ShapeDtypeStruct((n, d), F32),
                   jax.ShapeDtypeStruct((n, d // 2), jnp.uint32),
                   jax.ShapeDtypeStruct((n, LANES), jnp.int32),
                   jax.ShapeDtypeStruct((n, LANES), F32),
                   jax.ShapeDtypeStruct((n, LANES), jnp.int32),
                   jax.ShapeDtypeStruct((1, LANES), F32)),
        grid=(n // tm,),
        in_specs=[row(d), row(d), const(1, d), const(1, d), const(d, LANES), const(d, LANES),
                  const(1, LANES)],
        out_specs=(row(d), row(d // 2), row(LANES), row(LANES), row(LANES), const(1, LANES)),
        scratch_shapes=[pltpu.VMEM((1, LANES), F32)],
        compiler_params=_params(("arbitrary",)),
        name="ln_router",
    )(x, mix, g, b, rw_hi, rw_lo, rb)


def _expert_kernel(blk_e_ref, n_used_ref, row_tok_ref,
                   x_hbm, wgu_ref, bgu_ref, wdn_ref, bdn_ref, y_ref, xbuf, sem, *, bm, d_expert):
    i = pl.program_id(0)
    n_used = n_used_ref[0]

    def gather_copy(tok, r, slot):
        return pltpu.make_async_copy(x_hbm.at[pl.ds(tok, 1)], xbuf.at[slot, pl.ds(r, 1)], sem.at[slot])

    def issue(blk, slot):
        def body(r, _):
            gather_copy(row_tok_ref[blk * bm + r], r, slot).start()
            return 0
        lax.fori_loop(0, bm, body, 0, unroll=8)

    @pl.when(jnp.logical_and(i == 0, n_used > 0))
    def _():
        issue(0, 0)

    @pl.when(i + 1 < n_used)
    def _():
        issue(i + 1, (i + 1) % 2)

    @pl.when(i < n_used)
    def _():
        slot = i % 2

        def wait_body(r, _):
            gather_copy(0, r, slot).wait()
            return 0
        lax.fori_loop(0, bm, wait_body, 0, unroll=8)

        xb = _unpack_rows(xbuf[slot])
        hb = jnp.dot(xb, wgu_ref[...], preferred_element_type=F32) + bgu_ref[...]
        g = jnp.minimum(hb[:, :d_expert], SWIGLU_LIMIT)
        u = jnp.clip(hb[:, d_expert:], -SWIGLU_LIMIT, SWIGLU_LIMIT)
        act = (u + 1.0) * (g * (1.0 / (1.0 + jnp.exp(-SWIGLU_ALPHA * g))))
        y_ref[...] = jnp.dot(act.astype(BF16), wdn_ref[...], preferred_element_type=F32) + bdn_ref[...]

    @pl.when(i >= n_used)
    def _():
        y_ref[...] = jnp.zeros_like(y_ref)


def _experts(xp, blk_e, n_used, row_tok, w_gu, b_gu, w_dn, b_dn, layer, bm):
    _, _, d, f2 = w_gu.shape
    d_expert = f2 // 2
    n_rows = row_tok.shape[0]
    n_blk = n_rows // bm
    wmap = lambda i, be, nu, rt: (layer, be[i], 0, 0)
    grid_spec = pltpu.PrefetchScalarGridSpec(
        num_scalar_prefetch=3,
        grid=(n_blk,),
        in_specs=[pl.BlockSpec(memory_space=pl.ANY),
                  pl.BlockSpec((None, None, d, f2), wmap),
                  pl.BlockSpec((None, None, 1, f2), wmap),
                  pl.BlockSpec((None, None, d_expert, d), wmap),
                  pl.BlockSpec((None, None, 1, d), wmap)],
        out_specs=pl.BlockSpec((bm, d), lambda i, be, nu, rt: (i, 0)),
        scratch_shapes=[pltpu.VMEM((2, bm, d // 2), jnp.uint32), pltpu.SemaphoreType.DMA((2,))],
    )
    return pl.pallas_call(
        functools.partial(_expert_kernel, bm=bm, d_expert=d_expert),
        out_shape=jax.ShapeDtypeStruct((n_rows, d), F32),
        grid_spec=grid_spec,
        compiler_params=_params(("arbitrary",)),
        name="experts",
    )(blk_e, n_used, row_tok, xp, w_gu, b_gu[:, :, None, :], w_dn, b_dn[:, :, None, :])


def _combine_kernel(pos_ref, y_hbm, x1_ref, gate_ref, g_ref, b_ref, x2_ref, xb_ref, ybuf, sem, *, tm, alpha):
    i = pl.program_id(0)
    n_steps = pl.num_programs(0)

    def gather_copy(src_row, k, t, slot):
        return pltpu.make_async_copy(y_hbm.at[pl.ds(src_row, 1)], ybuf.at[slot, k, pl.ds(t, 1)], sem.at[slot])

    def issue(step, slot):
        def body(t, _):
            for k in range(TOP_K):
                gather_copy(pos_ref[(step * tm + t) * TOP_K + k], k, t, slot).start()
            return 0
        lax.fori_loop(0, tm, body, 0, unroll=2)

    @pl.when(i == 0)
    def _():
        issue(0, 0)

    @pl.when(i + 1 < n_steps)
    def _():
        issue(i + 1, (i + 1) % 2)

    slot = i % 2

    def wait_body(t, _):
        for k in range(TOP_K):
            gather_copy(0, k, t, slot).wait()
        return 0
    lax.fori_loop(0, tm, wait_body, 0, unroll=2)

    gate = gate_ref[...]
    ffn = gate[:, 0:1] * ybuf[slot, 0]
    for k in range(1, TOP_K):
        ffn = ffn + gate[:, k:k + 1] * ybuf[slot, k]
    x2 = _layer_norm(alpha * x1_ref[...] + ffn, g_ref[...], b_ref[...])
    x2_ref[...] = x2
    xb_ref[...] = x2.astype(BF16)


def _combine_ln(pos_flat, y, x1, gate, g, b, alpha):
    n, d = x1.shape
    tm = _pick_tile(n, (COMBINE_TM,))
    grid_spec = pltpu.PrefetchScalarGridSpec(
        num_scalar_prefetch=1,
        grid=(n // tm,),
        in_specs=[pl.BlockSpec(memory_space=pl.ANY),
                  pl.BlockSpec((tm, d), lambda i, p: (i, 0)),
                  pl.BlockSpec((tm, LANES), lambda i, p: (i, 0)),
                  pl.BlockSpec((1, d), lambda i, p: (0, 0)),
                  pl.BlockSpec((1, d), lambda i, p: (0, 0))],
        out_specs=(pl.BlockSpec((tm, d), lambda i, p: (i, 0)),
                   pl.BlockSpec((tm, d), lambda i, p: (i, 0))),
        scratch_shapes=[pltpu.VMEM((2, TOP_K, tm, d), F32), pltpu.SemaphoreType.DMA((2,))],
    )
    return pl.pallas_call(
        functools.partial(_combine_kernel, tm=tm, alpha=alpha),
        out_shape=(jax.ShapeDtypeStruct((n, d), F32), jax.ShapeDtypeStruct((n, d), BF16)),
        grid_spec=grid_spec,
        compiler_params=_params(("arbitrary",)),
        name="combine_ln",
    )(pos_flat, y, x1, gate, g, b)


def _pad_cols(w, width):
    return jnp.pad(w, ((0, 0), (0, width - w.shape[1])))


def _moe_layer(x, mix, ln_g, ln_b, router_w, router_b, w_gu, b_gu, w_dn, b_dn, layer, ln2_g, ln2_b, alpha):
    n, d = x.shape
    n_exp = router_w.shape[1]
    bm = EXPERT_BM

    rw = _pad_cols(router_w, LANES)
    rw_hi = rw.astype(BF16)
    rw_lo = (rw - rw_hi.astype(F32)).astype(BF16)
    rb = jnp.full((1, LANES), MASK_VALUE, F32).at[0, :n_exp].set(router_b)
    x1, xp, top_e, gate, rank, cnt = _ln_router(x, mix, ln_g[None], ln_b[None], rw_hi, rw_lo, rb, alpha)

    counts = cnt[0, :n_exp].astype(jnp.int32)
    padded = (counts + bm - 1) // bm * bm
    pend = jnp.cumsum(padded)
    pstart = pend - padded
    top_e = top_e[:, :TOP_K]
    pos = (pstart[top_e] + rank[:, :TOP_K]).reshape(-1)
    n_rows = -(-(n * TOP_K + n_exp * (bm - 1)) // bm) * bm
    n_blk = n_rows // bm
    tok = jnp.repeat(jnp.arange(n, dtype=jnp.int32), TOP_K)
    row_tok = jnp.zeros((n_rows,), jnp.int32).at[pos].set(tok)
    blk_e = jnp.minimum(jnp.searchsorted(pend, jnp.arange(n_blk, dtype=jnp.int32) * bm, side='right'),
                        n_exp - 1).astype(jnp.int32)
    n_used = (pend[-1:] // bm).astype(jnp.int32)

    y = _experts(xp, blk_e, n_used, row_tok, w_gu, b_gu, w_dn, b_dn, layer, bm)
    return _combine_ln(pos, y, x1, gate, ln2_g[None], ln2_b[None], alpha)


def _even_mixer(xb, w_in, b_f, w_o, batch, seq):
    sbw = SB_HEADS * HEAD_DIM
    fxw = FOX_HEADS * HEAD_DIM
    main = 3 * sbw + 3 * fxw
    zscale = HEAD_DIM ** -0.5 * LOG2E
    col = jnp.arange(main)
    is_q = (col < sbw) | ((col >= 3 * sbw) & (col < 3 * sbw + fxw))
    h = _matmul(xb, (w_in[:, :main] * jnp.where(is_q, zscale, 1.0)).astype(BF16), BF16)
    w_f = _pad_cols(w_in[:, main:], LANES).astype(BF16)
    cum = _fox_gate(xb, w_f, _pad_cols(b_f[None], LANES), batch, seq)
    o_sb = _sb_attention(h, batch, seq, SB_HEADS, 0, SB_HEADS, 2 * SB_HEADS)
    c0 = 3 * SB_HEADS
    o_fx = _fox_attention(h, cum, batch, seq, FOX_HEADS, c0, c0 + FOX_HEADS, c0 + 2 * FOX_HEADS)
    o = jnp.concatenate([o_sb, o_fx], axis=-1)
    return _matmul(o, w_o.astype(BF16), F32)


def _mla_mixer(xb, pos, w_down, q_norm, kv_norm, w_uq, w_ukv, w_o, batch, seq):
    hds = MLA_HEADS
    w_down_p = _pad_cols(w_down, Q_LORA + KV_LORA + LANES).astype(BF16)
    down = _matmul(xb, w_down_p, F32)
    freq = ROPE_THETA ** (-jnp.arange(0, ROPE_DIM, 2, dtype=F32) / ROPE_DIM)
    inv_freq = jnp.concatenate([freq, freq, jnp.zeros((LANES - ROPE_DIM,), F32)])[None]
    cq, ckv, k_pe, ra, rb, rc = _mla_prep(down, pos, q_norm[None], kv_norm[None], inv_freq)

    wq = w_uq.reshape(Q_LORA, hds, NOPE_DIM + ROPE_DIM)
    wq = jnp.pad(wq, ((0, 0), (0, 0), (0, LANES - ROPE_DIM))).reshape(Q_LORA, hds * (NOPE_DIM + LANES))
    wkv = w_ukv.reshape(KV_LORA, hds, NOPE_DIM + V_DIM)
    wkv = jnp.concatenate([wkv[:, :, :NOPE_DIM].reshape(KV_LORA, hds * NOPE_DIM),
                           wkv[:, :, NOPE_DIM:].reshape(KV_LORA, hds * V_DIM)], axis=1)
    q = _matmul(cq, (wq * ((NOPE_DIM + ROPE_DIM) ** -0.5 * LOG2E)).astype(BF16), BF16)
    kv = _matmul(ckv, wkv.astype(BF16), BF16)
    o = _mla_attention(q, kv, k_pe, ra, rb, rc, batch, seq, hds)
    return _matmul(o, w_o.astype(BF16), F32)


def kernel(x, positions, ln_mix_g, ln_mix_b, ln_ffn_g, ln_ffn_b, even_w_in, fox_b_f, even_w_o, mla_w_down, mla_q_norm, mla_kv_norm, mla_w_uq, mla_w_ukv, mla_w_o, router_w, router_b, expert_w_gate_up, expert_b_gate_up, expert_w_down, expert_b_down):
    batch, seq, d = x.shape
    depth = ln_mix_g.shape[0]
    alpha = (2 * depth) ** 0.25
    n = batch * seq
    xf = x.reshape(n, d)
    xb = xf.astype(BF16)
    pos = positions.reshape(n, 1)
    w_gu = expert_w_gate_up.astype(BF16)
    w_dn = expert_w_down.astype(BF16)
    for layer in range(depth):
        i = layer // 2
        if layer % 2 == 0:
            mix = _even_mixer(xb, even_w_in[i], fox_b_f[i], even_w_o[i], batch, seq)
        else:
            mix = _mla_mixer(xb, pos, mla_w_down[i], mla_q_norm[i], mla_kv_norm[i],
                             mla_w_uq[i], mla_w_ukv[i], mla_w_o[i], batch, seq)
        xf, xb = _moe_layer(xf, mix, ln_mix_g[layer], ln_mix_b[layer], router_w[layer], router_b[layer],
                            w_gu, expert_b_gate_up, w_dn, expert_b_down, layer,
                            ln_ffn_g[layer], ln_ffn_b[layer], alpha)
    return xf.reshape(batch, seq, d)
```

```python
import functools

import jax
import jax.numpy as jnp
from jax import lax
from jax.experimental import pallas as pl
from jax.experimental.pallas import tpu as pltpu
from jax.experimental.pallas import tpu_sc as plsc

F32 = jnp.float32
BF16 = jnp.bfloat16

HEAD_DIM = 128
SB_HEADS = 8
FOX_HEADS = 8
MLA_HEADS = 16
Q_LORA = 512
KV_LORA = 512
NOPE_DIM = 128
ROPE_DIM = 64
V_DIM = 128
ROPE_THETA = 10000.0
N_EXPERTS = 32
TOP_K = 4
SWIGLU_LIMIT = 7.0
SWIGLU_ALPHA = 1.702
LN_EPS = 1e-5
RMS_EPS = 1e-6

LANES = 128
MASK_VALUE = -1e30
VMEM_LIMIT = 56 * 1024 * 1024
LOG2E = 1.4426950408889634
LN2 = 0.6931471805599453

MM_TM = 1024
MM_TN = 512
SB_TQ = 512
SB_CK = 256
SB_GROUP = 4
FA_TILE = 512
FA_GROUP = 4
ROW_TILE = 256
GATE_TILE = 512
EXPERT_BM = 512
COMBINE_TM = 128


def _params(sem):
    return pltpu.CompilerParams(dimension_semantics=sem, vmem_limit_bytes=VMEM_LIMIT)


def _pick_tile(n, prefs):
    for t in prefs:
        if n % t == 0:
            return t
    return n


def _mm_kernel(a_ref, w_ref, o_ref):
    o_ref[...] = jnp.dot(a_ref[...].astype(BF16), w_ref[...],
                         preferred_element_type=F32).astype(o_ref.dtype)


def _matmul(a, w, out_dtype):
    m, k = a.shape
    n = w.shape[1]
    tm = _pick_tile(m, (MM_TM, 512, 256, 128))
    tn = _pick_tile(n, (MM_TN, 384, 256, 128))
    return pl.pallas_call(
        _mm_kernel,
        out_shape=jax.ShapeDtypeStruct((m, n), out_dtype),
        grid=(m // tm, n // tn),
        in_specs=[pl.BlockSpec((tm, k), lambda i, j: (i, 0)),
                  pl.BlockSpec((k, tn), lambda i, j: (0, j))],
        out_specs=pl.BlockSpec((tm, tn), lambda i, j: (i, j)),
        compiler_params=_params(("parallel", "parallel")),
        name="matmul",
    )(a, w)


def _split2(x):
    hi = x.astype(BF16)
    lo = (x - hi.astype(F32)).astype(BF16)
    return hi, lo


def _split3(x):
    h1 = x.astype(BF16)
    r1 = x - h1.astype(F32)
    h2 = r1.astype(BF16)
    h3 = (r1 - h2.astype(F32)).astype(BF16)
    return h1, h2, h3


def _log1pexp_neg_abs(z):
    return jnp.log(1.0 + jnp.exp(-jnp.abs(z)))


def _neg_abs(x):
    bits = lax.bitcast_convert_type(x, jnp.uint32) | jnp.uint32(0x80000000)
    return lax.bitcast_convert_type(bits, F32)


def _store_packed_rows(xp_ref, x):
    half = x.shape[1] // 2
    lo = lax.bitcast_convert_type(x[:, :half].astype(BF16).astype(F32), jnp.uint32)
    hi = lax.bitcast_convert_type(x[:, half:].astype(BF16).astype(F32), jnp.uint32)
    xp_ref[...] = (lo >> 16) | (hi & jnp.uint32(0xFFFF0000))


def _unpack_rows(w):
    lo = lax.bitcast_convert_type(w << 16, F32).astype(BF16)
    hi = lax.bitcast_convert_type(w & jnp.uint32(0xFFFF0000), F32).astype(BF16)
    return jnp.concatenate([lo, hi], axis=1)


def _layer_norm(h, g, b):
    mu = jnp.mean(h, axis=-1, keepdims=True)
    c = h - mu
    var = jnp.mean(c * c, axis=-1, keepdims=True)
    return c * lax.rsqrt(var + LN_EPS) * g + b


def _head_cols(hd):
    return slice(hd * HEAD_DIM, (hd + 1) * HEAD_DIM)


def _fox_gate_kernel(x_ref, w_ref, b_ref, o_ref, carry_ref):
    @pl.when(pl.program_id(1) == 0)
    def _():
        carry_ref[...] = jnp.zeros_like(carry_ref)

    f = jnp.dot(x_ref[...], w_ref[...], preferred_element_type=F32) + b_ref[...]
    log_f = jnp.minimum(f, 0.0) - _log1pexp_neg_abs(f)
    ts = log_f.shape[0]
    row = lax.broadcasted_iota(jnp.int32, (ts, ts), 0)
    col = lax.broadcasted_iota(jnp.int32, (ts, ts), 1)
    tri = jnp.where(col <= row, 1.0, 0.0).astype(BF16)
    cum = carry_ref[...]
    for part in _split3(log_f):
        cum = cum + jnp.dot(tri, part, preferred_element_type=F32)
    o_ref[...] = cum
    carry_ref[...] = cum[ts - 1:ts, :]


def _fox_gate(xb, w_f, b_f, batch, seq):
    n, d = xb.shape
    ts = _pick_tile(seq, (GATE_TILE, 256, 128))
    ns = seq // ts
    return pl.pallas_call(
        _fox_gate_kernel,
        out_shape=jax.ShapeDtypeStruct((n, LANES), F32),
        grid=(batch, ns),
        in_specs=[pl.BlockSpec((ts, d), lambda b, j: (b * ns + j, 0)),
                  pl.BlockSpec((d, LANES), lambda b, j: (0, 0)),
                  pl.BlockSpec((1, LANES), lambda b, j: (0, 0))],
        out_specs=pl.BlockSpec((ts, LANES), lambda b, j: (b * ns + j, 0)),
        scratch_shapes=[pltpu.VMEM((1, LANES), F32)],
        compiler_params=_params(("parallel", "arbitrary")),
        name="fox_gate",
    )(xb, w_f, b_f)


def _sb_kernel(q_ref, k_ref, v_ref, u_ref, o_ref, *, tq, ck, group):
    qi = pl.program_id(2)
    n_diag = tq // ck
    n_below = qi * n_diag

    def evaluate(units):
        n = len(units)
        starts = [pl.multiple_of(kb * ck, ck) for _, kb, _, _ in units]
        zs, sps, sums, stricts, laters = [None] * n, [None] * n, [None] * n, [None] * n, [None] * n
        pvs, tots = [None] * n, [None] * n

        def scores(u):
            hd, _, r0, masked = units[u]
            z = lax.dot_general(q_ref[r0:, _head_cols(hd)], k_ref[pl.ds(starts[u], ck), _head_cols(hd)],
                                (((1,), (1,)), ((), ())), preferred_element_type=F32)
            sp = jnp.maximum(z, 0.0) + jnp.log(1.0 + jnp.exp2(_neg_abs(z))) * (1.0 / LN2)
            zs[u], sps[u], sums[u] = z, sp, sp
            if masked:
                q_pos = qi * tq + r0 + lax.broadcasted_iota(jnp.int32, z.shape, 0)
                stricts[u] = (starts[u] + lax.broadcasted_iota(jnp.int32, z.shape, 1)) < q_pos
                sums[u] = jnp.where(stricts[u], sp, 0.0)

        def prefix(u):
            laters[u] = jnp.dot(sums[u].astype(BF16), u_ref[...], preferred_element_type=F32)

        def values(u):
            hd = units[u][0]
            w = jnp.exp2(zs[u] - sps[u] - laters[u])
            if stricts[u] is not None:
                w = jnp.where(stricts[u], w, 0.0)
            pvs[u] = jnp.dot(w.astype(BF16), v_ref[pl.ds(starts[u], ck), _head_cols(hd)],
                             preferred_element_type=F32)
            tots[u] = laters[u][:, :1] + sums[u][:, :1]

        for t in range(n + 2):
            if t >= 2:
                values(t - 2)
            if 1 <= t <= n:
                prefix(t - 1)
            if t < n:
                scores(t)
        return pvs, tots

    def fold(acc, later, pv, tot, r0):
        acc_s = acc[r0:] + pv * jnp.exp2(-later[r0:])
        later_s = later[r0:] + tot
        if r0:
            return (jnp.concatenate([acc[:r0], acc_s], axis=0),
                    jnp.concatenate([later[:r0], later_s], axis=0))
        return acc_s, later_s

    diag_units = [(hd, n_below + d, d * ck, True) for hd in range(group) for d in range(n_diag - 1, -1, -1)]
    pvs, tots = evaluate(diag_units)
    accs, laters = [], []
    for hd in range(group):
        acc, later = jnp.zeros((tq, HEAD_DIM), F32), jnp.zeros((tq, 1), F32)
        for u in range(hd * n_diag, (hd + 1) * n_diag):
            acc, later = fold(acc, later, pvs[u], tots[u], diag_units[u][2])
        accs.append(acc)
        laters.append(later)

    def body(i, carry):
        accs, laters = carry
        kb = n_below - 1 - 2 * i
        pvs, tots = evaluate([(hd, kb - c, 0, False) for hd in range(group) for c in range(2)])
        out = [fold(*fold(accs[hd], laters[hd], pvs[2 * hd], tots[2 * hd], 0),
                    pvs[2 * hd + 1], tots[2 * hd + 1], 0) for hd in range(group)]
        return tuple(o[0] for o in out), tuple(o[1] for o in out)

    accs, _ = lax.fori_loop(0, n_below // 2, body, (tuple(accs), tuple(laters)))
    for hd in range(group):
        o_ref[:, _head_cols(hd)] = accs[hd].astype(o_ref.dtype)


def _sb_attention(h, batch, seq, heads, q_col, k_col, v_col):
    n = h.shape[0]
    tq, ck = min(SB_TQ, seq), min(SB_CK, seq // 2)
    nq = seq // tq
    group = SB_GROUP
    assert (tq // ck) % 2 == 0 and heads % group == 0
    assert q_col % group == 0 and k_col % group == 0 and v_col % group == 0
    gw = group * HEAD_DIM
    j = lax.broadcasted_iota(jnp.int32, (ck, ck), 0)
    s = lax.broadcasted_iota(jnp.int32, (ck, ck), 1)
    u = jnp.where(j > s, 1.0, 0.0).astype(BF16)
    kern = functools.partial(_sb_kernel, tq=tq, ck=ck, group=group)
    return pl.pallas_call(
        kern,
        out_shape=jax.ShapeDtypeStruct((n, heads * HEAD_DIM), BF16),
        grid=(batch, heads // group, nq),
        in_specs=[pl.BlockSpec((tq, gw), lambda b, g, i: (b * nq + i, q_col // group + g)),
                  pl.BlockSpec((seq, gw), lambda b, g, i: (b, k_col // group + g)),
                  pl.BlockSpec((seq, gw), lambda b, g, i: (b, v_col // group + g)),
                  pl.BlockSpec((ck, ck), lambda b, g, i: (0, 0))],
        out_specs=pl.BlockSpec((tq, gw), lambda b, g, i: (b * nq + i, g)),
        compiler_params=_params(("parallel", "parallel", "arbitrary")),
        name="sb_attention",
    )(h, h, h, u)


def _flash_sweep(qs, k_fn, v_fn, bias_fn, qi, tile, dv):
    n = len(qs)

    def sweep_step(kb, states, masked):
        start = pl.multiple_of(kb * tile, tile)
        scores, new = [None] * n, [None] * n

        def score(hd):
            s = lax.dot_general(qs[hd], k_fn(hd, start), (((1,), (1,)), ((), ())), preferred_element_type=F32)
            if bias_fn is not None:
                s = s + bias_fn(hd, kb)
            if masked:
                s = jnp.where(lax.broadcasted_iota(jnp.int32, s.shape, 1)
                              <= lax.broadcasted_iota(jnp.int32, s.shape, 0), s, MASK_VALUE)
            scores[hd] = s

        def update(hd):
            m, l, acc = states[hd]
            s = scores[hd]
            m_new = jnp.maximum(m, jnp.max(s, axis=-1, keepdims=True))
            p = jnp.exp2(s - m_new)
            corr = jnp.exp2(m - m_new)
            l = corr * l + jnp.sum(p, axis=-1, keepdims=True)
            acc = corr * acc + jnp.dot(p.astype(BF16), v_fn(hd, start), preferred_element_type=F32)
            new[hd] = (m_new, l, acc)

        for i in range(n + 1):
            if i < n:
                score(i)
            if i >= 1:
                update(i - 1)
        return tuple(new)

    init = (jnp.full((tile, 1), MASK_VALUE, F32), jnp.zeros((tile, 1), F32), jnp.zeros((tile, dv), F32))
    states = lax.fori_loop(0, qi, lambda kb, st: sweep_step(kb, st, False), tuple(init for _ in range(n)))
    states = sweep_step(qi, states, True)
    return [acc / l for _, l, acc in states]


def _fox_kernel(q_ref, k_ref, v_ref, ck_ref, o_ref, *, tile, group):
    qs = [q_ref[:, _head_cols(hd)] for hd in range(group)]
    outs = _flash_sweep(
        qs,
        lambda hd, start: k_ref[pl.ds(start, tile), _head_cols(hd)],
        lambda hd, start: v_ref[pl.ds(start, tile), _head_cols(hd)],
        lambda hd, kb: ck_ref[hd, pl.ds(kb, 1), :] * (-LOG2E),
        pl.program_id(2), tile, HEAD_DIM)
    for hd in range(group):
        o_ref[:, _head_cols(hd)] = outs[hd].astype(o_ref.dtype)


def _fox_attention(h, cum, batch, seq, heads, q_col, k_col, v_col):
    n = h.shape[0]
    tile = min(FA_TILE, seq)
    nq = seq // tile
    group = FA_GROUP
    assert heads % group == 0 and q_col % group == 0 and k_col % group == 0 and v_col % group == 0
    gw = group * HEAD_DIM
    cum_k = jnp.transpose(cum[:, :heads].reshape(batch, seq, heads), (0, 2, 1)).reshape(batch, heads, nq, tile)
    kern = functools.partial(_fox_kernel, tile=tile, group=group)
    return pl.pallas_call(
        kern,
        out_shape=jax.ShapeDtypeStruct((n, heads * HEAD_DIM), BF16),
        grid=(batch, heads // group, nq),
        in_specs=[pl.BlockSpec((tile, gw), lambda b, g, i: (b * nq + i, q_col // group + g)),
                  pl.BlockSpec((seq, gw), lambda b, g, i: (b, k_col // group + g)),
                  pl.BlockSpec((seq, gw), lambda b, g, i: (b, v_col // group + g)),
                  pl.BlockSpec((None, group, nq, tile), lambda b, g, i: (b, g, 0, 0))],
        out_specs=pl.BlockSpec((tile, gw), lambda b, g, i: (b * nq + i, g)),
        compiler_params=_params(("parallel", "parallel", "arbitrary")),
        name="fox_attention",
    )(h, h, h, cum_k)


def _rope_lanes(t, ra, rb, rc):
    half = ROPE_DIM // 2
    return t * ra + pltpu.roll(t, half, 1) * rb + pltpu.roll(t, LANES - half, 1) * rc


def _mla_kernel(q_ref, kn_ref, kp_ref, v_ref, ra_ref, rb_ref, rc_ref, o_ref, kfull_ref,
                *, tile, group):
    dq = NOPE_DIM + LANES

    @pl.when(pl.program_id(2) == 0)
    def _():
        for hd in range(group):
            kfull_ref[:, hd * dq:hd * dq + NOPE_DIM] = kn_ref[:, _head_cols(hd)]
            kfull_ref[:, hd * dq + NOPE_DIM:(hd + 1) * dq] = kp_ref[...]

    qs = []
    for hd in range(group):
        q = q_ref[:, hd * dq:(hd + 1) * dq]
        q_pe = _rope_lanes(q[:, NOPE_DIM:].astype(F32), ra_ref[...], rb_ref[...], rc_ref[...])
        qs.append(jnp.concatenate([q[:, :NOPE_DIM], q_pe.astype(BF16)], axis=1))
    outs = _flash_sweep(
        qs,
        lambda hd, start: kfull_ref[pl.ds(start, tile), hd * dq:(hd + 1) * dq],
        lambda hd, start: v_ref[pl.ds(start, tile), _head_cols(hd)],
        None, pl.program_id(2), tile, V_DIM)
    for hd in range(group):
        o_ref[:, _head_cols(hd)] = outs[hd].astype(o_ref.dtype)


def _mla_attention(q, kv, k_pe, ra, rb, rc, batch, seq, heads):
    n = q.shape[0]
    tile = min(FA_TILE, seq)
    nq = seq // tile
    group = FA_GROUP
    assert heads % group == 0
    ng = heads // group
    dq = NOPE_DIM + LANES
    kern = functools.partial(_mla_kernel, tile=tile, group=group)
    rope_spec = pl.BlockSpec((tile, LANES), lambda b, g, i: (b * nq + i, 0))
    return pl.pallas_call(
        kern,
        out_shape=jax.ShapeDtypeStruct((n, heads * V_DIM), BF16),
        grid=(batch, ng, nq),
        in_specs=[pl.BlockSpec((tile, group * dq), lambda b, g, i: (b * nq + i, g)),
                  pl.BlockSpec((seq, group * NOPE_DIM), lambda b, g, i: (b, g)),
                  pl.BlockSpec((seq, LANES), lambda b, g, i: (b, 0)),
                  pl.BlockSpec((seq, group * V_DIM), lambda b, g, i: (b, ng + g)),
                  rope_spec, rope_spec, rope_spec],
        out_specs=pl.BlockSpec((tile, group * V_DIM), lambda b, g, i: (b * nq + i, g)),
        scratch_shapes=[pltpu.VMEM((seq, group * dq), BF16)],
        compiler_params=_params(("parallel", "parallel", "arbitrary")),
        name="mla_attention",
    )(q, kv, k_pe, kv, ra, rb, rc)


def _mla_prep_kernel(down_ref, pos_ref, qn_ref, kvn_ref, invf_ref,
                     cq_ref, ckv_ref, kpe_ref, ra_ref, rb_ref, rc_ref):
    down = down_ref[...]

    def rms(c, g):
        return c * lax.rsqrt(jnp.mean(c * c, axis=-1, keepdims=True) + RMS_EPS) * g

    cq_ref[...] = rms(down[:, :Q_LORA], qn_ref[...]).astype(cq_ref.dtype)
    ckv_ref[...] = rms(down[:, Q_LORA:Q_LORA + KV_LORA], kvn_ref[...]).astype(ckv_ref.dtype)

    ang = pos_ref[...].astype(F32) * invf_ref[...]
    cos, sin = jnp.cos(ang), jnp.sin(ang)
    lane = lax.broadcasted_iota(jnp.int32, ang.shape, 1)
    half = ROPE_DIM // 2
    ra = jnp.where(lane < ROPE_DIM, cos, 0.0)
    rb = jnp.where((lane >= half) & (lane < ROPE_DIM), sin, 0.0)
    rc = jnp.where(lane < half, -sin, 0.0)
    ra_ref[...] = ra
    rb_ref[...] = rb
    rc_ref[...] = rc
    kpe_ref[...] = _rope_lanes(down[:, Q_LORA + KV_LORA:], ra, rb, rc).astype(kpe_ref.dtype)


def _mla_prep(down, pos, q_norm, kv_norm, inv_freq):
    n, dd = down.shape
    tm = _pick_tile(n, (ROW_TILE, 128))
    row = lambda w: pl.BlockSpec((tm, w), lambda i: (i, 0))
    const = lambda w: pl.BlockSpec((1, w), lambda i: (0, 0))
    return pl.pallas_call(
        _mla_prep_kernel,
        out_shape=(jax.ShapeDtypeStruct((n, Q_LORA), BF16),
                   jax.ShapeDtypeStruct((n, KV_LORA), BF16),
                   jax.ShapeDtypeStruct((n, LANES), BF16),
                   jax.ShapeDtypeStruct((n, LANES), F32),
                   jax.ShapeDtypeStruct((n, LANES), F32),
                   jax.ShapeDtypeStruct((n, LANES), F32)),
        grid=(n // tm,),
        in_specs=[row(dd), row(1), const(Q_LORA), const(KV_LORA), const(LANES)],
        out_specs=(row(Q_LORA), row(KV_LORA), row(LANES), row(LANES), row(LANES), row(LANES)),
        compiler_params=_params(("parallel",)),
        name="mla_prep",
    )(down, pos, q_norm, kv_norm, inv_freq)


def _ln_router_kernel(x_ref, mix_ref, g_ref, b_ref, rw_hi_ref, rw_lo_ref, rb_ref,
                      x1_ref, xp_ref, e_ref, gate_ref, rank_ref, cnt_ref, carry_ref, *, alpha):
    @pl.when(pl.program_id(0) == 0)
    def _():
        carry_ref[...] = jnp.zeros_like(carry_ref)

    x1 = _layer_norm(alpha * x_ref[...] + mix_ref[...], g_ref[...], b_ref[...])
    x1_ref[...] = x1
    _store_packed_rows(xp_ref, x1)

    a_hi, a_lo = _split2(x1)
    rw_hi, rw_lo = rw_hi_ref[...], rw_lo_ref[...]
    logits = (jnp.dot(a_hi, rw_hi, preferred_element_type=F32)
              + jnp.dot(a_hi, rw_lo, preferred_element_type=F32)
              + jnp.dot(a_lo, rw_hi, preferred_element_type=F32)) + rb_ref[...]

    tm = logits.shape[0]
    lane = lax.broadcasted_iota(jnp.int32, (tm, LANES), 1)
    work = logits
    picks, vals = [], []
    for _ in range(TOP_K):
        mx = jnp.max(work, axis=-1, keepdims=True)
        idx = jnp.min(jnp.where(work == mx, lane, LANES), axis=-1, keepdims=True)
        pick = lane == idx
        picks.append(pick)
        vals.append(mx)
        work = jnp.where(pick, -jnp.inf, work)

    exps = [jnp.exp(v - vals[0]) for v in vals]
    denom = exps[0]
    for e in exps[1:]:
        denom = denom + e

    chosen = jnp.zeros((tm, LANES), F32)
    for pick in picks:
        chosen = chosen + jnp.where(pick, 1.0, 0.0)
    row = lax.broadcasted_iota(jnp.int32, (tm, tm), 0)
    col = lax.broadcasted_iota(jnp.int32, (tm, tm), 1)
    strict_lower = jnp.where(col < row, 1.0, 0.0).astype(BF16)
    before = jnp.dot(strict_lower, chosen.astype(BF16), preferred_element_type=F32) + carry_ref[...]

    e_out = jnp.zeros((tm, LANES), jnp.int32)
    gate_out = jnp.zeros((tm, LANES), F32)
    rank_out = jnp.zeros((tm, LANES), F32)
    for k in range(TOP_K):
        idx = jnp.min(jnp.where(picks[k], lane, LANES), axis=-1, keepdims=True)
        rank = jnp.sum(jnp.where(picks[k], before, 0.0), axis=-1, keepdims=True)
        e_out = jnp.where(lane == k, idx, e_out)
        gate_out = jnp.where(lane == k, exps[k] / denom, gate_out)
        rank_out = jnp.where(lane == k, rank, rank_out)
    e_ref[...] = e_out
    gate_ref[...] = gate_out
    rank_ref[...] = rank_out.astype(jnp.int32)

    carry_ref[...] = carry_ref[...] + jnp.sum(chosen, axis=0, keepdims=True)
    cnt_ref[...] = carry_ref[...]


def _ln_router(x, mix, g, b, rw_hi, rw_lo, rb, alpha):
    n, d = x.shape
    tm = _pick_tile(n, (ROW_TILE, 128))
    row = lambda w: pl.BlockSpec((tm, w), lambda i: (i, 0))
    const = lambda r, w: pl.BlockSpec((r, w), lambda i: (0, 0))
    return pl.pallas_call(
        functools.partial(_ln_router_kernel, alpha=alpha),
        out_shape=(jax.ShapeDtypeStruct((n, d), F32),
                   jax.ShapeDtypeStruct((n, d // 2), jnp.uint32),
                   jax.ShapeDtypeStruct((n, LANES), jnp.int32),
                   jax.ShapeDtypeStruct((n, LANES), F32),
                   jax.ShapeDtypeStruct((n, LANES), jnp.int32),
                   jax.ShapeDtypeStruct((1, LANES), F32)),
        grid=(n // tm,),
        in_specs=[row(d), row(d), const(1, d), const(1, d), const(d, LANES), const(d, LANES),
                  const(1, LANES)],
        out_specs=(row(d), row(d // 2), row(LANES), row(LANES), row(LANES), const(1, LANES)),
        scratch_shapes=[pltpu.VMEM((1, LANES), F32)],
        compiler_params=_params(("arbitrary",)),
        name="ln_router",
    )(x, mix, g, b, rw_hi, rw_lo, rb)


SC_CORES = 2
SC_SUBCORES = 16
SC_CHUNK = 32


def _sc_gather_rows(table, idx):
    b = idx.shape[0]
    width = table.shape[1]
    workers = SC_CORES * SC_SUBCORES
    per_worker = b // workers
    assert b % (workers * SC_CHUNK) == 0
    n_chunks = per_worker // SC_CHUNK
    mesh = plsc.VectorSubcoreMesh(core_axis_name="c", subcore_axis_name="s",
                                  num_cores=SC_CORES, num_subcores=SC_SUBCORES)

    @functools.partial(
        pl.kernel, mesh=mesh,
        out_type=jax.ShapeDtypeStruct((b, width), table.dtype),
        scratch_types=[pltpu.VMEM((per_worker,), jnp.int32),
                       pltpu.VMEM((2, SC_CHUNK, width), table.dtype),
                       pltpu.SemaphoreType.DMA((2,))],
        name="sc_gather_rows",
    )
    def gather(table_hbm, idx_hbm, out_hbm, idx_v, rows_v, sem):
        wid = lax.axis_index("s") * SC_CORES + lax.axis_index("c")
        base = wid * per_worker
        pltpu.sync_copy(idx_hbm.at[pl.ds(base, per_worker)], idx_v)

        def start(j, slot):
            pltpu.async_copy(table_hbm.at[idx_v.at[pl.ds(j * SC_CHUNK, SC_CHUNK)]], rows_v.at[slot], sem.at[slot])

        def wait(slot):
            pltpu.make_async_copy(table_hbm.at[pl.ds(0, SC_CHUNK)], rows_v.at[slot], sem.at[slot]).wait()

        start(0, 0)

        def body(j, _):
            slot = j % 2

            @pl.when(j + 1 < n_chunks)
            def _():
                start(j + 1, 1 - slot)

            wait(slot)
            pltpu.sync_copy(rows_v.at[slot], out_hbm.at[pl.ds(base + j * SC_CHUNK, SC_CHUNK)])
            return 0

        lax.fori_loop(0, n_chunks, body, 0)

    return gather(table, idx)


def _expert_kernel(blk_e_ref, n_used_ref, xs_ref, wgu_ref, bgu_ref, wdn_ref, bdn_ref, y_ref, *, d_expert):
    i = pl.program_id(0)

    @pl.when(i < n_used_ref[0])
    def _():
        xb = _unpack_rows(xs_ref[...])
        hb = jnp.dot(xb, wgu_ref[...], preferred_element_type=F32) + bgu_ref[...]
        g = jnp.minimum(hb[:, :d_expert], SWIGLU_LIMIT)
        u = jnp.clip(hb[:, d_expert:], -SWIGLU_LIMIT, SWIGLU_LIMIT)
        act = (u + 1.0) * (g * (1.0 / (1.0 + jnp.exp(-SWIGLU_ALPHA * g))))
        y_ref[...] = jnp.dot(act.astype(BF16), wdn_ref[...], preferred_element_type=F32) + bdn_ref[...]

    @pl.when(i >= n_used_ref[0])
    def _():
        y_ref[...] = jnp.zeros_like(y_ref)


def _experts(xs, blk_e, n_used, w_gu, b_gu, w_dn, b_dn, layer, bm):
    _, _, d, f2 = w_gu.shape
    d_expert = f2 // 2
    n_rows = xs.shape[0]
    n_blk = n_rows // bm
    wmap = lambda i, be, nu: (layer, be[i], 0, 0)
    xmap = lambda i, be, nu: (jnp.minimum(i, jnp.maximum(nu[0] - 1, 0)), 0)
    grid_spec = pltpu.PrefetchScalarGridSpec(
        num_scalar_prefetch=2,
        grid=(n_blk,),
        in_specs=[pl.BlockSpec((bm, d // 2), xmap),
                  pl.BlockSpec((None, None, d, f2), wmap),
                  pl.BlockSpec((None, None, 1, f2), wmap),
                  pl.BlockSpec((None, None, d_expert, d), wmap),
                  pl.BlockSpec((None, None, 1, d), wmap)],
        out_specs=pl.BlockSpec((bm, d), lambda i, be, nu: (i, 0)),
    )
    return pl.pallas_call(
        functools.partial(_expert_kernel, d_expert=d_expert),
        out_shape=jax.ShapeDtypeStruct((n_rows, d), F32),
        grid_spec=grid_spec,
        compiler_params=_params(("arbitrary",)),
        name="experts",
    )(blk_e, n_used, xs, w_gu, b_gu[:, :, None, :], w_dn, b_dn[:, :, None, :])


def _combine_kernel(pos_ref, y_hbm, x1_ref, gate_ref, g_ref, b_ref, x2_ref, xb_ref, ybuf, sem, *, tm, alpha):
    i = pl.program_id(0)
    n_steps = pl.num_programs(0)

    def gather_copy(src_row, k, t, slot):
        return pltpu.make_async_copy(y_hbm.at[pl.ds(src_row, 1)], ybuf.at[slot, k, pl.ds(t, 1)], sem.at[slot])

    def issue(step, slot):
        def body(t, _):
            for k in range(TOP_K):
                gather_copy(pos_ref[(step * tm + t) * TOP_K + k], k, t, slot).start()
            return 0
        lax.fori_loop(0, tm, body, 0, unroll=2)

    @pl.when(i == 0)
    def _():
        issue(0, 0)

    @pl.when(i + 1 < n_steps)
    def _():
        issue(i + 1, (i + 1) % 2)

    slot = i % 2

    def wait_body(t, _):
        for k in range(TOP_K):
            gather_copy(0, k, t, slot).wait()
        return 0
    lax.fori_loop(0, tm, wait_body, 0, unroll=2)

    gate = gate_ref[...]
    ffn = gate[:, 0:1] * ybuf[slot, 0]
    for k in range(1, TOP_K):
        ffn = ffn + gate[:, k:k + 1] * ybuf[slot, k]
    x2 = _layer_norm(alpha * x1_ref[...] + ffn, g_ref[...], b_ref[...])
    x2_ref[...] = x2
    xb_ref[...] = x2.astype(BF16)


def _combine_ln(pos_flat, y, x1, gate, g, b, alpha):
    n, d = x1.shape
    tm = _pick_tile(n, (COMBINE_TM,))
    grid_spec = pltpu.PrefetchScalarGridSpec(
        num_scalar_prefetch=1,
        grid=(n // tm,),
        in_specs=[pl.BlockSpec(memory_space=pl.ANY),
                  pl.BlockSpec((tm, d), lambda i, p: (i, 0)),
                  pl.BlockSpec((tm, LANES), lambda i, p: (i, 0)),
                  pl.BlockSpec((1, d), lambda i, p: (0, 0)),
                  pl.BlockSpec((1, d), lambda i, p: (0, 0))],
        out_specs=(pl.BlockSpec((tm, d), lambda i, p: (i, 0)),
                   pl.BlockSpec((tm, d), lambda i, p: (i, 0))),
        scratch_shapes=[pltpu.VMEM((2, TOP_K, tm, d), F32), pltpu.SemaphoreType.DMA((2,))],
    )
    return pl.pallas_call(
        functools.partial(_combine_kernel, tm=tm, alpha=alpha),
        out_shape=(jax.ShapeDtypeStruct((n, d), F32), jax.ShapeDtypeStruct((n, d), BF16)),
        grid_spec=grid_spec,
        compiler_params=_params(("arbitrary",)),
        name="combine_ln",
    )(pos_flat, y, x1, gate, g, b)


def _pad_cols(w, width):
    return jnp.pad(w, ((0, 0), (0, width - w.shape[1])))


def _moe_layer(x, mix, ln_g, ln_b, router_w, router_b, w_gu, b_gu, w_dn, b_dn, layer, ln2_g, ln2_b, alpha):
    n, d = x.shape
    n_exp = router_w.shape[1]
    bm = EXPERT_BM

    rw = _pad_cols(router_w, LANES)
    rw_hi = rw.astype(BF16)
    rw_lo = (rw - rw_hi.astype(F32)).astype(BF16)
    rb = jnp.full((1, LANES), MASK_VALUE, F32).at[0, :n_exp].set(router_b)
    x1, xp, top_e, gate, rank, cnt = _ln_router(x, mix, ln_g[None], ln_b[None], rw_hi, rw_lo, rb, alpha)

    counts = cnt[0, :n_exp].astype(jnp.int32)
    padded = (counts + bm - 1) // bm * bm
    pend = jnp.cumsum(padded)
    pstart = pend - padded
    top_e = top_e[:, :TOP_K]
    pos = (pstart[top_e] + rank[:, :TOP_K]).reshape(-1)
    n_rows = -(-(n * TOP_K + n_exp * (bm - 1)) // bm) * bm
    n_blk = n_rows // bm
    tok = jnp.repeat(jnp.arange(n, dtype=jnp.int32), TOP_K)
    row_tok = jnp.zeros((n_rows,), jnp.int32).at[pos].set(tok)
    blk_e = jnp.minimum(jnp.searchsorted(pend, jnp.arange(n_blk, dtype=jnp.int32) * bm, side='right'),
                        n_exp - 1).astype(jnp.int32)
    n_used = (pend[-1:] // bm).astype(jnp.int32)

    xs = _sc_gather_rows(lax.bitcast_convert_type(xp, jnp.int32), row_tok)
    y = _experts(lax.bitcast_convert_type(xs, jnp.uint32), blk_e, n_used, w_gu, b_gu, w_dn, b_dn, layer, bm)
    return _combine_ln(pos, y, x1, gate, ln2_g[None], ln2_b[None], alpha)


def _even_mixer(xb, w_in, b_f, w_o, batch, seq):
    sbw = SB_HEADS * HEAD_DIM
    fxw = FOX_HEADS * HEAD_DIM
    main = 3 * sbw + 3 * fxw
    zscale = HEAD_DIM ** -0.5 * LOG2E
    col = jnp.arange(main)
    is_q = (col < sbw) | ((col >= 3 * sbw) & (col < 3 * sbw + fxw))
    h = _matmul(xb, (w_in[:, :main] * jnp.where(is_q, zscale, 1.0)).astype(BF16), BF16)
    w_f = _pad_cols(w_in[:, main:], LANES).astype(BF16)
    cum = _fox_gate(xb, w_f, _pad_cols(b_f[None], LANES), batch, seq)
    o_sb = _sb_attention(h, batch, seq, SB_HEADS, 0, SB_HEADS, 2 * SB_HEADS)
    c0 = 3 * SB_HEADS
    o_fx = _fox_attention(h, cum, batch, seq, FOX_HEADS, c0, c0 + FOX_HEADS, c0 + 2 * FOX_HEADS)
    o = jnp.concatenate([o_sb, o_fx], axis=-1)
    return _matmul(o, w_o.astype(BF16), F32)


def _mla_mixer(xb, pos, w_down, q_norm, kv_norm, w_uq, w_ukv, w_o, batch, seq):
    hds = MLA_HEADS
    w_down_p = _pad_cols(w_down, Q_LORA + KV_LORA + LANES).astype(BF16)
    down = _matmul(xb, w_down_p, F32)
    freq = ROPE_THETA ** (-jnp.arange(0, ROPE_DIM, 2, dtype=F32) / ROPE_DIM)
    inv_freq = jnp.concatenate([freq, freq, jnp.zeros((LANES - ROPE_DIM,), F32)])[None]
    cq, ckv, k_pe, ra, rb, rc = _mla_prep(down, pos, q_norm[None], kv_norm[None], inv_freq)

    wq = w_uq.reshape(Q_LORA, hds, NOPE_DIM + ROPE_DIM)
    wq = jnp.pad(wq, ((0, 0), (0, 0), (0, LANES - ROPE_DIM))).reshape(Q_LORA, hds * (NOPE_DIM + LANES))
    wkv = w_ukv.reshape(KV_LORA, hds, NOPE_DIM + V_DIM)
    wkv = jnp.concatenate([wkv[:, :, :NOPE_DIM].reshape(KV_LORA, hds * NOPE_DIM),
                           wkv[:, :, NOPE_DIM:].reshape(KV_LORA, hds * V_DIM)], axis=1)
    q = _matmul(cq, (wq * ((NOPE_DIM + ROPE_DIM) ** -0.5 * LOG2E)).astype(BF16), BF16)
    kv = _matmul(ckv, wkv.astype(BF16), BF16)
    o = _mla_attention(q, kv, k_pe, ra, rb, rc, batch, seq, hds)
    return _matmul(o, w_o.astype(BF16), F32)


def kernel(x, positions, ln_mix_g, ln_mix_b, ln_ffn_g, ln_ffn_b, even_w_in, fox_b_f, even_w_o, mla_w_down, mla_q_norm, mla_kv_norm, mla_w_uq, mla_w_ukv, mla_w_o, router_w, router_b, expert_w_gate_up, expert_b_gate_up, expert_w_down, expert_b_down):
    batch, seq, d = x.shape
    depth = ln_mix_g.shape[0]
    alpha = (2 * depth) ** 0.25
    n = batch * seq
    xf = x.reshape(n, d)
    xb = xf.astype(BF16)
    pos = positions.reshape(n, 1)
    w_gu = expert_w_gate_up.astype(BF16)
    w_dn = expert_w_down.astype(BF16)
    for layer in range(depth):
        i = layer // 2
        if layer % 2 == 0:
            mix = _even_mixer(xb, even_w_in[i], fox_b_f[i], even_w_o[i], batch, seq)
        else:
            mix = _mla_mixer(xb, pos, mla_w_down[i], mla_q_norm[i], mla_kv_norm[i],
                             mla_w_uq[i], mla_w_ukv[i], mla_w_o[i], batch, seq)
        xf, xb = _moe_layer(xf, mix, ln_mix_g[layer], ln_mix_b[layer], router_w[layer], router_b[layer],
                            w_gu, expert_b_gate_up, w_dn, expert_b_down, layer,
                            ln_ffn_g[layer], ln_ffn_b[layer], alpha)
    return xf.reshape(batch, seq, d)
```

```python
import functools

import jax
import jax.numpy as jnp
from jax import lax
from jax.experimental import pallas as pl
from jax.experimental.pallas import tpu as pltpu

F32 = jnp.float32
BF16 = jnp.bfloat16

HEAD_DIM = 128
SB_HEADS = 8
FOX_HEADS = 8
MLA_HEADS = 16
Q_LORA = 512
KV_LORA = 512
NOPE_DIM = 128
ROPE_DIM = 64
V_DIM = 128
ROPE_THETA = 10000.0
N_EXPERTS = 32
TOP_K = 4
SWIGLU_LIMIT = 7.0
SWIGLU_ALPHA = 1.702
LN_EPS = 1e-5
RMS_EPS = 1e-6

LANES = 128
MASK_VALUE = -1e30
VMEM_LIMIT = 56 * 1024 * 1024
LOG2E = 1.4426950408889634
LN2 = 0.6931471805599453

MM_TM = 1024
MM_TN = 512
SB_TQ = 512
SB_CK = 256
SB_GROUP = 4
FA_TILE = 512
FA_GROUP = 4
ROW_TILE = 256
GATE_TILE = 512
EXPERT_BM = 512
COMBINE_TM = 128


def _params(sem):
    return pltpu.CompilerParams(dimension_semantics=sem, vmem_limit_bytes=VMEM_LIMIT)


def _pick_tile(n, prefs):
    for t in prefs:
        if n % t == 0:
            return t
    return n


def _mm_kernel(a_ref, w_ref, o_ref):
    o_ref[...] = jnp.dot(a_ref[...].astype(BF16), w_ref[...],
                         preferred_element_type=F32).astype(o_ref.dtype)


def _matmul(a, w, out_dtype):
    m, k = a.shape
    n = w.shape[1]
    tm = _pick_tile(m, (MM_TM, 512, 256, 128))
    tn = _pick_tile(n, (MM_TN, 384, 256, 128))
    return pl.pallas_call(
        _mm_kernel,
        out_shape=jax.ShapeDtypeStruct((m, n), out_dtype),
        grid=(m // tm, n // tn),
        in_specs=[pl.BlockSpec((tm, k), lambda i, j: (i, 0)),
                  pl.BlockSpec((k, tn), lambda i, j: (0, j))],
        out_specs=pl.BlockSpec((tm, tn), lambda i, j: (i, j)),
        compiler_params=_params(("parallel", "parallel")),
        name="matmul",
    )(a, w)


def _split2(x):
    hi = x.astype(BF16)
    lo = (x - hi.astype(F32)).astype(BF16)
    return hi, lo


def _split3(x):
    h1 = x.astype(BF16)
    r1 = x - h1.astype(F32)
    h2 = r1.astype(BF16)
    h3 = (r1 - h2.astype(F32)).astype(BF16)
    return h1, h2, h3


def _log1pexp_neg_abs(z):
    return jnp.log(1.0 + jnp.exp(-jnp.abs(z)))


def _neg_abs(x):
    bits = lax.bitcast_convert_type(x, jnp.uint32) | jnp.uint32(0x80000000)
    return lax.bitcast_convert_type(bits, F32)


def _store_packed_rows(xp_ref, x):
    half = x.shape[1] // 2
    lo = lax.bitcast_convert_type(x[:, :half].astype(BF16).astype(F32), jnp.uint32)
    hi = lax.bitcast_convert_type(x[:, half:].astype(BF16).astype(F32), jnp.uint32)
    xp_ref[...] = (lo >> 16) | (hi & jnp.uint32(0xFFFF0000))


def _unpack_rows(w):
    lo = lax.bitcast_convert_type(w << 16, F32).astype(BF16)
    hi = lax.bitcast_convert_type(w & jnp.uint32(0xFFFF0000), F32).astype(BF16)
    return jnp.concatenate([lo, hi], axis=1)


def _layer_norm(h, g, b):
    mu = jnp.mean(h, axis=-1, keepdims=True)
    c = h - mu
    var = jnp.mean(c * c, axis=-1, keepdims=True)
    return c * lax.rsqrt(var + LN_EPS) * g + b


def _head_cols(hd):
    return slice(hd * HEAD_DIM, (hd + 1) * HEAD_DIM)


def _fox_gate_kernel(x_ref, w_ref, b_ref, o_ref, carry_ref):
    @pl.when(pl.program_id(1) == 0)
    def _():
        carry_ref[...] = jnp.zeros_like(carry_ref)

    f = jnp.dot(x_ref[...], w_ref[...], preferred_element_type=F32) + b_ref[...]
    log_f = jnp.minimum(f, 0.0) - _log1pexp_neg_abs(f)
    ts = log_f.shape[0]
    row = lax.broadcasted_iota(jnp.int32, (ts, ts), 0)
    col = lax.broadcasted_iota(jnp.int32, (ts, ts), 1)
    tri = jnp.where(col <= row, 1.0, 0.0).astype(BF16)
    cum = carry_ref[...]
    for part in _split3(log_f):
        cum = cum + jnp.dot(tri, part, preferred_element_type=F32)
    o_ref[...] = cum
    carry_ref[...] = cum[ts - 1:ts, :]


def _fox_gate(xb, w_f, b_f, batch, seq):
    n, d = xb.shape
    ts = _pick_tile(seq, (GATE_TILE, 256, 128))
    ns = seq // ts
    return pl.pallas_call(
        _fox_gate_kernel,
        out_shape=jax.ShapeDtypeStruct((n, LANES), F32),
        grid=(batch, ns),
        in_specs=[pl.BlockSpec((ts, d), lambda b, j: (b * ns + j, 0)),
                  pl.BlockSpec((d, LANES), lambda b, j: (0, 0)),
                  pl.BlockSpec((1, LANES), lambda b, j: (0, 0))],
        out_specs=pl.BlockSpec((ts, LANES), lambda b, j: (b * ns + j, 0)),
        scratch_shapes=[pltpu.VMEM((1, LANES), F32)],
        compiler_params=_params(("parallel", "arbitrary")),
        name="fox_gate",
    )(xb, w_f, b_f)


def _sb_kernel(q_ref, k_ref, v_ref, u_ref, o_ref, *, tq, ck, group):
    qi = pl.program_id(2)
    n_diag = tq // ck
    n_below = qi * n_diag

    def evaluate(units):
        n = len(units)
        starts = [pl.multiple_of(kb * ck, ck) for _, kb, _, _ in units]
        zs, sps, sums, stricts, laters = [None] * n, [None] * n, [None] * n, [None] * n, [None] * n
        pvs, tots = [None] * n, [None] * n

        def scores(u):
            hd, _, r0, masked = units[u]
            z = lax.dot_general(q_ref[r0:, _head_cols(hd)], k_ref[pl.ds(starts[u], ck), _head_cols(hd)],
                                (((1,), (1,)), ((), ())), preferred_element_type=F32)
            sp = jnp.maximum(z, 0.0) + jnp.log(1.0 + jnp.exp2(_neg_abs(z))) * (1.0 / LN2)
            zs[u], sps[u], sums[u] = z, sp, sp
            if masked:
                q_pos = qi * tq + r0 + lax.broadcasted_iota(jnp.int32, z.shape, 0)
                stricts[u] = (starts[u] + lax.broadcasted_iota(jnp.int32, z.shape, 1)) < q_pos
                sums[u] = jnp.where(stricts[u], sp, 0.0)

        def prefix(u):
            laters[u] = jnp.dot(sums[u].astype(BF16), u_ref[...], preferred_element_type=F32)

        def values(u):
            hd = units[u][0]
            w = jnp.exp2(zs[u] - sps[u] - laters[u])
            if stricts[u] is not None:
                w = jnp.where(stricts[u], w, 0.0)
            pvs[u] = jnp.dot(w.astype(BF16), v_ref[pl.ds(starts[u], ck), _head_cols(hd)],
                             preferred_element_type=F32)
            tots[u] = laters[u][:, :1] + sums[u][:, :1]

        for t in range(n + 2):
            if t >= 2:
                values(t - 2)
            if 1 <= t <= n:
                prefix(t - 1)
            if t < n:
                scores(t)
        return pvs, tots

    def fold(acc, later, pv, tot, r0):
        acc_s = acc[r0:] + pv * jnp.exp2(-later[r0:])
        later_s = later[r0:] + tot
        if r0:
            return (jnp.concatenate([acc[:r0], acc_s], axis=0),
                    jnp.concatenate([later[:r0], later_s], axis=0))
        return acc_s, later_s

    diag_units = [(hd, n_below + d, d * ck, True) for hd in range(group) for d in range(n_diag - 1, -1, -1)]
    pvs, tots = evaluate(diag_units)
    accs, laters = [], []
    for hd in range(group):
        acc, later = jnp.zeros((tq, HEAD_DIM), F32), jnp.zeros((tq, 1), F32)
        for u in range(hd * n_diag, (hd + 1) * n_diag):
            acc, later = fold(acc, later, pvs[u], tots[u], diag_units[u][2])
        accs.append(acc)
        laters.append(later)

    def body(i, carry):
        accs, laters = carry
        kb = n_below - 1 - 2 * i
        pvs, tots = evaluate([(hd, kb - c, 0, False) for hd in range(group) for c in range(2)])
        out = [fold(*fold(accs[hd], laters[hd], pvs[2 * hd], tots[2 * hd], 0),
                    pvs[2 * hd + 1], tots[2 * hd + 1], 0) for hd in range(group)]
        return tuple(o[0] for o in out), tuple(o[1] for o in out)

    accs, _ = lax.fori_loop(0, n_below // 2, body, (tuple(accs), tuple(laters)))
    for hd in range(group):
        o_ref[:, _head_cols(hd)] = accs[hd].astype(o_ref.dtype)


def _sb_attention(h, batch, seq, heads, q_col, k_col, v_col):
    n = h.shape[0]
    tq, ck = min(SB_TQ, seq), min(SB_CK, seq // 2)
    nq = seq // tq
    group = SB_GROUP
    assert (tq // ck) % 2 == 0 and heads % group == 0
    assert q_col % group == 0 and k_col % group == 0 and v_col % group == 0
    gw = group * HEAD_DIM
    j = lax.broadcasted_iota(jnp.int32, (ck, ck), 0)
    s = lax.broadcasted_iota(jnp.int32, (ck, ck), 1)
    u = jnp.where(j > s, 1.0, 0.0).astype(BF16)
    kern = functools.partial(_sb_kernel, tq=tq, ck=ck, group=group)
    return pl.pallas_call(
        kern,
        out_shape=jax.ShapeDtypeStruct((n, heads * HEAD_DIM), BF16),
        grid=(batch, heads // group, nq),
        in_specs=[pl.BlockSpec((tq, gw), lambda b, g, i: (b * nq + i, q_col // group + g)),
                  pl.BlockSpec((seq, gw), lambda b, g, i: (b, k_col // group + g)),
                  pl.BlockSpec((seq, gw), lambda b, g, i: (b, v_col // group + g)),
                  pl.BlockSpec((ck, ck), lambda b, g, i: (0, 0))],
        out_specs=pl.BlockSpec((tq, gw), lambda b, g, i: (b * nq + i, g)),
        compiler_params=_params(("parallel", "parallel", "arbitrary")),
        name="sb_attention",
    )(h, h, h, u)


def _flash_sweep(qs, k_fn, v_fn, bias_fn, qi, tile, dv):
    n = len(qs)

    def sweep_step(kb, states, masked):
        start = pl.multiple_of(kb * tile, tile)
        scores, new = [None] * n, [None] * n

        def score(hd):
            s = lax.dot_general(qs[hd], k_fn(hd, start), (((1,), (1,)), ((), ())), preferred_element_type=F32)
            if bias_fn is not None:
                s = s + bias_fn(hd, kb)
            if masked:
                s = jnp.where(lax.broadcasted_iota(jnp.int32, s.shape, 1)
                              <= lax.broadcasted_iota(jnp.int32, s.shape, 0), s, MASK_VALUE)
            scores[hd] = s

        def update(hd):
            m, l, acc = states[hd]
            s = scores[hd]
            m_new = jnp.maximum(m, jnp.max(s, axis=-1, keepdims=True))
            p = jnp.exp2(s - m_new)
            corr = jnp.exp2(m - m_new)
            l = corr * l + jnp.sum(p, axis=-1, keepdims=True)
            acc = corr * acc + jnp.dot(p.astype(BF16), v_fn(hd, start), preferred_element_type=F32)
            new[hd] = (m_new, l, acc)

        for i in range(n + 1):
            if i < n:
                score(i)
            if i >= 1:
                update(i - 1)
        return tuple(new)

    init = (jnp.full((tile, 1), MASK_VALUE, F32), jnp.zeros((tile, 1), F32), jnp.zeros((tile, dv), F32))
    states = lax.fori_loop(0, qi, lambda kb, st: sweep_step(kb, st, False), tuple(init for _ in range(n)))
    states = sweep_step(qi, states, True)
    return [acc / l for _, l, acc in states]


def _fox_kernel(q_ref, k_ref, v_ref, ck_ref, o_ref, *, tile, group):
    qs = [q_ref[:, _head_cols(hd)] for hd in range(group)]
    outs = _flash_sweep(
        qs,
        lambda hd, start: k_ref[pl.ds(start, tile), _head_cols(hd)],
        lambda hd, start: v_ref[pl.ds(start, tile), _head_cols(hd)],
        lambda hd, kb: ck_ref[hd, pl.ds(kb, 1), :] * (-LOG2E),
        pl.program_id(2), tile, HEAD_DIM)
    for hd in range(group):
        o_ref[:, _head_cols(hd)] = outs[hd].astype(o_ref.dtype)


def _fox_attention(h, cum, batch, seq, heads, q_col, k_col, v_col):
    n = h.shape[0]
    tile = min(FA_TILE, seq)
    nq = seq // tile
    group = FA_GROUP
    assert heads % group == 0 and q_col % group == 0 and k_col % group == 0 and v_col % group == 0
    gw = group * HEAD_DIM
    cum_k = jnp.transpose(cum[:, :heads].reshape(batch, seq, heads), (0, 2, 1)).reshape(batch, heads, nq, tile)
    kern = functools.partial(_fox_kernel, tile=tile, group=group)
    return pl.pallas_call(
        kern,
        out_shape=jax.ShapeDtypeStruct((n, heads * HEAD_DIM), BF16),
        grid=(batch, heads // group, nq),
        in_specs=[pl.BlockSpec((tile, gw), lambda b, g, i: (b * nq + i, q_col // group + g)),
                  pl.BlockSpec((seq, gw), lambda b, g, i: (b, k_col // group + g)),
                  pl.BlockSpec((seq, gw), lambda b, g, i: (b, v_col // group + g)),
                  pl.BlockSpec((None, group, nq, tile), lambda b, g, i: (b, g, 0, 0))],
        out_specs=pl.BlockSpec((tile, gw), lambda b, g, i: (b * nq + i, g)),
        compiler_params=_params(("parallel", "parallel", "arbitrary")),
        name="fox_attention",
    )(h, h, h, cum_k)


def _rope_lanes(t, ra, rb, rc):
    half = ROPE_DIM // 2
    return t * ra + pltpu.roll(t, half, 1) * rb + pltpu.roll(t, LANES - half, 1) * rc


def _mla_kernel(q_ref, kn_ref, kp_ref, v_ref, ra_ref, rb_ref, rc_ref, o_ref, kfull_ref,
                *, tile, group):
    dq = NOPE_DIM + LANES

    @pl.when(pl.program_id(2) == 0)
    def _():
        for hd in range(group):
            kfull_ref[:, hd * dq:hd * dq + NOPE_DIM] = kn_ref[:, _head_cols(hd)]
            kfull_ref[:, hd * dq + NOPE_DIM:(hd + 1) * dq] = kp_ref[...]

    qs = []
    for hd in range(group):
        q = q_ref[:, hd * dq:(hd + 1) * dq]
        q_pe = _rope_lanes(q[:, NOPE_DIM:].astype(F32), ra_ref[...], rb_ref[...], rc_ref[...])
        qs.append(jnp.concatenate([q[:, :NOPE_DIM], q_pe.astype(BF16)], axis=1))
    outs = _flash_sweep(
        qs,
        lambda hd, start: kfull_ref[pl.ds(start, tile), hd * dq:(hd + 1) * dq],
        lambda hd, start: v_ref[pl.ds(start, tile), _head_cols(hd)],
        None, pl.program_id(2), tile, V_DIM)
    for hd in range(group):
        o_ref[:, _head_cols(hd)] = outs[hd].astype(o_ref.dtype)


def _mla_attention(q, kv, k_pe, ra, rb, rc, batch, seq, heads):
    n = q.shape[0]
    tile = min(FA_TILE, seq)
    nq = seq // tile
    group = FA_GROUP
    assert heads % group == 0
    ng = heads // group
    dq = NOPE_DIM + LANES
    kern = functools.partial(_mla_kernel, tile=tile, group=group)
    rope_spec = pl.BlockSpec((tile, LANES), lambda b, g, i: (b * nq + i, 0))
    return pl.pallas_call(
        kern,
        out_shape=jax.ShapeDtypeStruct((n, heads * V_DIM), BF16),
        grid=(batch, ng, nq),
        in_specs=[pl.BlockSpec((tile, group * dq), lambda b, g, i: (b * nq + i, g)),
                  pl.BlockSpec((seq, group * NOPE_DIM), lambda b, g, i: (b, g)),
                  pl.BlockSpec((seq, LANES), lambda b, g, i: (b, 0)),
                  pl.BlockSpec((seq, group * V_DIM), lambda b, g, i: (b, ng + g)),
                  rope_spec, rope_spec, rope_spec],
        out_specs=pl.BlockSpec((tile, group * V_DIM), lambda b, g, i: (b * nq + i, g)),
        scratch_shapes=[pltpu.VMEM((seq, group * dq), BF16)],
        compiler_params=_params(("parallel", "parallel", "arbitrary")),
        name="mla_attention",
    )(q, kv, k_pe, kv, ra, rb, rc)


def _mla_prep_kernel(down_ref, pos_ref, qn_ref, kvn_ref, invf_ref,
                     cq_ref, ckv_ref, kpe_ref, ra_ref, rb_ref, rc_ref):
    down = down_ref[...]

    def rms(c, g):
        return c * lax.rsqrt(jnp.mean(c * c, axis=-1, keepdims=True) + RMS_EPS) * g

    cq_ref[...] = rms(down[:, :Q_LORA], qn_ref[...]).astype(cq_ref.dtype)
    ckv_ref[...] = rms(down[:, Q_LORA:Q_LORA + KV_LORA], kvn_ref[...]).astype(ckv_ref.dtype)

    ang = pos_ref[...].astype(F32) * invf_ref[...]
    cos, sin = jnp.cos(ang), jnp.sin(ang)
    lane = lax.broadcasted_iota(jnp.int32, ang.shape, 1)
    half = ROPE_DIM // 2
    ra = jnp.where(lane < ROPE_DIM, cos, 0.0)
    rb = jnp.where((lane >= half) & (lane < ROPE_DIM), sin, 0.0)
    rc = jnp.where(lane < half, -sin, 0.0)
    ra_ref[...] = ra
    rb_ref[...] = rb
    rc_ref[...] = rc
    kpe_ref[...] = _rope_lanes(down[:, Q_LORA + KV_LORA:], ra, rb, rc).astype(kpe_ref.dtype)


def _mla_prep(down, pos, q_norm, kv_norm, inv_freq):
    n, dd = down.shape
    tm = _pick_tile(n, (ROW_TILE, 128))
    row = lambda w: pl.BlockSpec((tm, w), lambda i: (i, 0))
    const = lambda w: pl.BlockSpec((1, w), lambda i: (0, 0))
    return pl.pallas_call(
        _mla_prep_kernel,
        out_shape=(jax.ShapeDtypeStruct((n, Q_LORA), BF16),
                   jax.ShapeDtypeStruct((n, KV_LORA), BF16),
                   jax.ShapeDtypeStruct((n, LANES), BF16),
                   jax.ShapeDtypeStruct((n, LANES), F32),
                   jax.ShapeDtypeStruct((n, LANES), F32),
                   jax.ShapeDtypeStruct((n, LANES), F32)),
        grid=(n // tm,),
        in_specs=[row(dd), row(1), const(Q_LORA), const(KV_LORA), const(LANES)],
        out_specs=(row(Q_LORA), row(KV_LORA), row(LANES), row(LANES), row(LANES), row(LANES)),
        compiler_params=_params(("parallel",)),
        name="mla_prep",
    )(down, pos, q_norm, kv_norm, inv_freq)


def _ln_router_kernel(x_ref, mix_ref, g_ref, b_ref, rw_hi_ref, rw_lo_ref, rb_ref,
                      x1_ref, xp_ref, e_ref, gate_ref, rank_ref, cnt_ref, carry_ref, *, alpha):
    @pl.when(pl.program_id(0) == 0)
    def _():
        carry_ref[...] = jnp.zeros_like(carry_ref)

    x1 = _layer_norm(alpha * x_ref[...] + mix_ref[...], g_ref[...], b_ref[...])
    x1_ref[...] = x1
    _store_packed_rows(xp_ref, x1)

    a_hi, a_lo = _split2(x1)
    rw_hi, rw_lo = rw_hi_ref[...], rw_lo_ref[...]
    logits = (jnp.dot(a_hi, rw_hi, preferred_element_type=F32)
              + jnp.dot(a_hi, rw_lo, preferred_element_type=F32)
              + jnp.dot(a_lo, rw_hi, preferred_element_type=F32)) + rb_ref[...]

    tm = logits.shape[0]
    lane = lax.broadcasted_iota(jnp.int32, (tm, LANES), 1)
    work = logits
    picks, vals = [], []
    for _ in range(TOP_K):
        mx = jnp.max(work, axis=-1, keepdims=True)
        idx = jnp.min(jnp.where(work == mx, lane, LANES), axis=-1, keepdims=True)
        pick = lane == idx
        picks.append(pick)
        vals.append(mx)
        work = jnp.where(pick, -jnp.inf, work)

    exps = [jnp.exp(v - vals[0]) for v in vals]
    denom = exps[0]
    for e in exps[1:]:
        denom = denom + e

    chosen = jnp.zeros((tm, LANES), F32)
    for pick in picks:
        chosen = chosen + jnp.where(pick, 1.0, 0.0)
    row = lax.broadcasted_iota(jnp.int32, (tm, tm), 0)
    col = lax.broadcasted_iota(jnp.int32, (tm, tm), 1)
    strict_lower = jnp.where(col < row, 1.0, 0.0).astype(BF16)
    before = jnp.dot(strict_lower, chosen.astype(BF16), preferred_element_type=F32) + carry_ref[...]

    e_out = jnp.zeros((tm, LANES), jnp.int32)
    gate_out = jnp.zeros((tm, LANES), F32)
    rank_out = jnp.zeros((tm, LANES), F32)
    for k in range(TOP_K):
        idx = jnp.min(jnp.where(picks[k], lane, LANES), axis=-1, keepdims=True)
        rank = jnp.sum(jnp.where(picks[k], before, 0.0), axis=-1, keepdims=True)
        e_out = jnp.where(lane == k, idx, e_out)
        gate_out = jnp.where(lane == k, exps[k] / denom, gate_out)
        rank_out = jnp.where(lane == k, rank, rank_out)
    e_ref[...] = e_out
    gate_ref[...] = gate_out
    rank_ref[...] = rank_out.astype(jnp.int32)

    carry_ref[...] = carry_ref[...] + jnp.sum(chosen, axis=0, keepdims=True)
    cnt_ref[...] = carry_ref[...]


def _ln_router(x, mix, g, b, rw_hi, rw_lo, rb, alpha):
    n, d = x.shape
    tm = _pick_tile(n, (ROW_TILE, 128))
    row = lambda w: pl.BlockSpec((tm, w), lambda i: (i, 0))
    const = lambda r, w: pl.BlockSpec((r, w), lambda i: (0, 0))
    return pl.pallas_call(
        functools.partial(_ln_router_kernel, alpha=alpha),
        out_shape=(jax.ShapeDtypeStruct((n, d), F32),
                   jax.ShapeDtypeStruct((n, d // 2), jnp.uint32),
                   jax.ShapeDtypeStruct((n, LANES), jnp.int32),
                   jax.ShapeDtypeStruct((n, LANES), F32),
                   jax.ShapeDtypeStruct((n, LANES), jnp.int32),
                   jax.ShapeDtypeStruct((1, LANES), F32)),
        grid=(n // tm,),
        in_specs=[row(d), row(d), const(1, d), const(1, d), const(d, LANES), const(d, LANES),
                  const(1, LANES)],
        out_specs=(row(d), row(d // 2), row(LANES), row(LANES), row(LANES), const(1, LANES)),
        scratch_shapes=[pltpu.VMEM((1, LANES), F32)],
        compiler_params=_params(("arbitrary",)),
        name="ln_router",
    )(x, mix, g, b, rw_hi, rw_lo, rb)


def _expert_kernel(blk_e_ref, n_used_ref, row_tok_ref,
                   x_hbm, wgu_ref, bgu_ref, wdn_ref, bdn_ref, y_ref, xbuf, sem, *, bm, d_expert):
    i = pl.program_id(0)
    n_used = n_used_ref[0]
    d = y_ref.shape[1]
    chunk = 2 * LANES
    n_up, n_dn = 2 * d_expert // chunk, d // chunk
    per_gap = -(-bm // (n_up + n_dn))

    def gather_copy(tok, r, slot):
        return pltpu.make_async_copy(x_hbm.at[pl.ds(tok, 1)], xbuf.at[slot, pl.ds(r, 1)], sem.at[slot])

    def wait_all(slot):
        def wait_body(r, _):
            gather_copy(0, r, slot).wait()
            return 0
        lax.fori_loop(0, bm, wait_body, 0, unroll=8)

    @pl.when(jnp.logical_and(i == 0, n_used > 0))
    def _():
        def body(r, _):
            gather_copy(row_tok_ref[r], r, 0).start()
            return 0
        lax.fori_loop(0, bm, body, 0, unroll=8)

    @pl.when(i < n_used)
    def _():
        slot = i % 2
        wait_all(slot)
        nxt = jnp.minimum(i + 1, n_used - 1) * bm
        issued = [0]

        def issue_some():
            for r in range(issued[0], min(issued[0] + per_gap, bm)):
                gather_copy(row_tok_ref[nxt + r], r, 1 - slot).start()
            issued[0] = min(issued[0] + per_gap, bm)

        xb = _unpack_rows(xbuf[slot])
        hbs = []
        for c in range(n_up):
            hbs.append(jnp.dot(xb, wgu_ref[:, c * chunk:(c + 1) * chunk], preferred_element_type=F32)
                       + bgu_ref[:, c * chunk:(c + 1) * chunk])
            issue_some()
        hb = jnp.concatenate(hbs, axis=1)
        g = jnp.minimum(hb[:, :d_expert], SWIGLU_LIMIT)
        u = jnp.clip(hb[:, d_expert:], -SWIGLU_LIMIT, SWIGLU_LIMIT)
        act = ((u + 1.0) * (g * (1.0 / (1.0 + jnp.exp(-SWIGLU_ALPHA * g))))).astype(BF16)
        for c in range(n_dn):
            y_ref[:, c * chunk:(c + 1) * chunk] = (
                jnp.dot(act, wdn_ref[:, c * chunk:(c + 1) * chunk], preferred_element_type=F32)
                + bdn_ref[:, c * chunk:(c + 1) * chunk])
            issue_some()

    @pl.when(jnp.logical_and(i == n_used, n_used > 0))
    def _():
        wait_all(i % 2)

    @pl.when(i >= n_used)
    def _():
        y_ref[...] = jnp.zeros_like(y_ref)


def _experts(xp, blk_e, n_used, row_tok, w_gu, b_gu, w_dn, b_dn, layer, bm):
    _, _, d, f2 = w_gu.shape
    d_expert = f2 // 2
    n_rows = row_tok.shape[0]
    n_blk = n_rows // bm
    wmap = lambda i, be, nu, rt: (layer, be[i], 0, 0)
    grid_spec = pltpu.PrefetchScalarGridSpec(
        num_scalar_prefetch=3,
        grid=(n_blk,),
        in_specs=[pl.BlockSpec(memory_space=pl.ANY),
                  pl.BlockSpec((None, None, d, f2), wmap),
                  pl.BlockSpec((None, None, 1, f2), wmap),
                  pl.BlockSpec((None, None, d_expert, d), wmap),
                  pl.BlockSpec((None, None, 1, d), wmap)],
        out_specs=pl.BlockSpec((bm, d), lambda i, be, nu, rt: (i, 0)),
        scratch_shapes=[pltpu.VMEM((2, bm, d // 2), jnp.uint32), pltpu.SemaphoreType.DMA((2,))],
    )
    return pl.pallas_call(
        functools.partial(_expert_kernel, bm=bm, d_expert=d_expert),
        out_shape=jax.ShapeDtypeStruct((n_rows, d), F32),
        grid_spec=grid_spec,
        compiler_params=_params(("arbitrary",)),
        name="experts",
    )(blk_e, n_used, row_tok, xp, w_gu, b_gu[:, :, None, :], w_dn, b_dn[:, :, None, :])


def _combine_kernel(pos_ref, y_hbm, x1_ref, gate_ref, g_ref, b_ref, x2_ref, xb_ref, ybuf, sem, *, tm, alpha):
    i = pl.program_id(0)
    n_steps = pl.num_programs(0)

    def gather_copy(src_row, k, t, slot):
        return pltpu.make_async_copy(y_hbm.at[pl.ds(src_row, 1)], ybuf.at[slot, k, pl.ds(t, 1)], sem.at[slot])

    def issue(step, slot):
        def body(t, _):
            for k in range(TOP_K):
                gather_copy(pos_ref[(step * tm + t) * TOP_K + k], k, t, slot).start()
            return 0
        lax.fori_loop(0, tm, body, 0, unroll=2)

    @pl.when(i == 0)
    def _():
        issue(0, 0)

    @pl.when(i + 1 < n_steps)
    def _():
        issue(i + 1, (i + 1) % 2)

    slot = i % 2

    def wait_body(t, _):
        for k in range(TOP_K):
            gather_copy(0, k, t, slot).wait()
        return 0
    lax.fori_loop(0, tm, wait_body, 0, unroll=2)

    gate = gate_ref[...]
    ffn = gate[:, 0:1] * ybuf[slot, 0]
    for k in range(1, TOP_K):
        ffn = ffn + gate[:, k:k + 1] * ybuf[slot, k]
    x2 = _layer_norm(alpha * x1_ref[...] + ffn, g_ref[...], b_ref[...])
    x2_ref[...] = x2
    xb_ref[...] = x2.astype(BF16)


def _combine_ln(pos_flat, y, x1, gate, g, b, alpha):
    n, d = x1.shape
    tm = _pick_tile(n, (COMBINE_TM,))
    grid_spec = pltpu.PrefetchScalarGridSpec(
        num_scalar_prefetch=1,
        grid=(n // tm,),
        in_specs=[pl.BlockSpec(memory_space=pl.ANY),
                  pl.BlockSpec((tm, d), lambda i, p: (i, 0)),
                  pl.BlockSpec((tm, LANES), lambda i, p: (i, 0)),
                  pl.BlockSpec((1, d), lambda i, p: (0, 0)),
                  pl.BlockSpec((1, d), lambda i, p: (0, 0))],
        out_specs=(pl.BlockSpec((tm, d), lambda i, p: (i, 0)),
                   pl.BlockSpec((tm, d), lambda i, p: (i, 0))),
        scratch_shapes=[pltpu.VMEM((2, TOP_K, tm, d), F32), pltpu.SemaphoreType.DMA((2,))],
    )
    return pl.pallas_call(
        functools.partial(_combine_kernel, tm=tm, alpha=alpha),
        out_shape=(jax.ShapeDtypeStruct((n, d), F32), jax.ShapeDtypeStruct((n, d), BF16)),
        grid_spec=grid_spec,
        compiler_params=_params(("arbitrary",)),
        name="combine_ln",
    )(pos_flat, y, x1, gate, g, b)


def _pad_cols(w, width):
    return jnp.pad(w, ((0, 0), (0, width - w.shape[1])))


def _moe_layer(x, mix, ln_g, ln_b, router_w, router_b, w_gu, b_gu, w_dn, b_dn, layer, ln2_g, ln2_b, alpha):
    n, d = x.shape
    n_exp = router_w.shape[1]
    bm = EXPERT_BM

    rw = _pad_cols(router_w, LANES)
    rw_hi = rw.astype(BF16)
    rw_lo = (rw - rw_hi.astype(F32)).astype(BF16)
    rb = jnp.full((1, LANES), MASK_VALUE, F32).at[0, :n_exp].set(router_b)
    x1, xp, top_e, gate, rank, cnt = _ln_router(x, mix, ln_g[None], ln_b[None], rw_hi, rw_lo, rb, alpha)

    counts = cnt[0, :n_exp].astype(jnp.int32)
    padded = (counts + bm - 1) // bm * bm
    pend = jnp.cumsum(padded)
    pstart = pend - padded
    top_e = top_e[:, :TOP_K]
    pos = (pstart[top_e] + rank[:, :TOP_K]).reshape(-1)
    n_blk = -(-(n * TOP_K + n_exp * (bm - 1)) // bm) + 1
    n_rows = n_blk * bm
    tok = jnp.repeat(jnp.arange(n, dtype=jnp.int32), TOP_K)
    row_tok = jnp.zeros((n_rows,), jnp.int32).at[pos].set(tok)
    blk_start = jnp.arange(n_blk, dtype=jnp.int32) * bm
    blk_e = jnp.minimum(jnp.sum((pend[None, :] <= blk_start[:, None]).astype(jnp.int32), axis=1), n_exp - 1)
    n_used = (pend[-1:] // bm).astype(jnp.int32)

    y = _experts(xp, blk_e, n_used, row_tok, w_gu, b_gu, w_dn, b_dn, layer, bm)
    return _combine_ln(pos, y, x1, gate, ln2_g[None], ln2_b[None], alpha)


def _even_mixer(xb, w_in, b_f, w_o, batch, seq):
    sbw = SB_HEADS * HEAD_DIM
    fxw = FOX_HEADS * HEAD_DIM
    main = 3 * sbw + 3 * fxw
    zscale = HEAD_DIM ** -0.5 * LOG2E
    col = jnp.arange(main)
    is_q = (col < sbw) | ((col >= 3 * sbw) & (col < 3 * sbw + fxw))
    h = _matmul(xb, (w_in[:, :main] * jnp.where(is_q, zscale, 1.0)).astype(BF16), BF16)
    w_f = _pad_cols(w_in[:, main:], LANES).astype(BF16)
    cum = _fox_gate(xb, w_f, _pad_cols(b_f[None], LANES), batch, seq)
    o_sb = _sb_attention(h, batch, seq, SB_HEADS, 0, SB_HEADS, 2 * SB_HEADS)
    c0 = 3 * SB_HEADS
    o_fx = _fox_attention(h, cum, batch, seq, FOX_HEADS, c0, c0 + FOX_HEADS, c0 + 2 * FOX_HEADS)
    o = jnp.concatenate([o_sb, o_fx], axis=-1)
    return _matmul(o, w_o.astype(BF16), F32)


def _mla_mixer(xb, pos, w_down, q_norm, kv_norm, w_uq, w_ukv, w_o, batch, seq):
    hds = MLA_HEADS
    w_down_p = _pad_cols(w_down, Q_LORA + KV_LORA + LANES).astype(BF16)
    down = _matmul(xb, w_down_p, F32)
    freq = ROPE_THETA ** (-jnp.arange(0, ROPE_DIM, 2, dtype=F32) / ROPE_DIM)
    inv_freq = jnp.concatenate([freq, freq, jnp.zeros((LANES - ROPE_DIM,), F32)])[None]
    cq, ckv, k_pe, ra, rb, rc = _mla_prep(down, pos, q_norm[None], kv_norm[None], inv_freq)

    wq = w_uq.reshape(Q_LORA, hds, NOPE_DIM + ROPE_DIM)
    wq = jnp.pad(wq, ((0, 0), (0, 0), (0, LANES - ROPE_DIM))).reshape(Q_LORA, hds * (NOPE_DIM + LANES))
    wkv = w_ukv.reshape(KV_LORA, hds, NOPE_DIM + V_DIM)
    wkv = jnp.concatenate([wkv[:, :, :NOPE_DIM].reshape(KV_LORA, hds * NOPE_DIM),
                           wkv[:, :, NOPE_DIM:].reshape(KV_LORA, hds * V_DIM)], axis=1)
    q = _matmul(cq, (wq * ((NOPE_DIM + ROPE_DIM) ** -0.5 * LOG2E)).astype(BF16), BF16)
    kv = _matmul(ckv, wkv.astype(BF16), BF16)
    o = _mla_attention(q, kv, k_pe, ra, rb, rc, batch, seq, hds)
    return _matmul(o, w_o.astype(BF16), F32)


def kernel(x, positions, ln_mix_g, ln_mix_b, ln_ffn_g, ln_ffn_b, even_w_in, fox_b_f, even_w_o, mla_w_down, mla_q_norm, mla_kv_norm, mla_w_uq, mla_w_ukv, mla_w_o, router_w, router_b, expert_w_gate_up, expert_b_gate_up, expert_w_down, expert_b_down):
    batch, seq, d = x.shape
    depth = ln_mix_g.shape[0]
    alpha = (2 * depth) ** 0.25
    n = batch * seq
    xf = x.reshape(n, d)
    xb = xf.astype(BF16)
    pos = positions.reshape(n, 1)
    w_gu = expert_w_gate_up.astype(BF16)
    w_dn = expert_w_down.astype(BF16)
    for layer in range(depth):
        i = layer // 2
        if layer % 2 == 0:
            mix = _even_mixer(xb, even_w_in[i], fox_b_f[i], even_w_o[i], batch, seq)
        else:
            mix = _mla_mixer(xb, pos, mla_w_down[i], mla_q_norm[i], mla_kv_norm[i],
                             mla_w_uq[i], mla_w_ukv[i], mla_w_o[i], batch, seq)
        xf, xb = _moe_layer(xf, mix, ln_mix_g[layer], ln_mix_b[layer], router_w[layer], router_b[layer],
                            w_gu, expert_b_gate_up, w_dn, expert_b_down, layer,
                            ln_ffn_g[layer], ln_ffn_b[layer], alpha)
    return xf.reshape(batch, seq, d)
```

```python
import functools

import jax
import jax.numpy as jnp
from jax import lax
from jax.experimental import pallas as pl
from jax.experimental.pallas import tpu as pltpu
from jax.experimental.pallas import tpu_sc as plsc

F32 = jnp.float32
BF16 = jnp.bfloat16

HEAD_DIM = 128
SB_HEADS = 8
FOX_HEADS = 8
MLA_HEADS = 16
Q_LORA = 512
KV_LORA = 512
NOPE_DIM = 128
ROPE_DIM = 64
V_DIM = 128
ROPE_THETA = 10000.0
N_EXPERTS = 32
TOP_K = 4
SWIGLU_LIMIT = 7.0
SWIGLU_ALPHA = 1.702
LN_EPS = 1e-5
RMS_EPS = 1e-6

LANES = 128
MASK_VALUE = -1e30
VMEM_LIMIT = 56 * 1024 * 1024
LOG2E = 1.4426950408889634
LN2 = 0.6931471805599453

MM_TM = 1024
MM_TN = 512
SB_TQ = 512
SB_CK = 256
SB_GROUP = 4
FA_TILE = 512
FA_GROUP = 4
ROW_TILE = 256
GATE_TILE = 512
EXPERT_BM = 512
COMBINE_TM = 128


def _params(sem):
    return pltpu.CompilerParams(dimension_semantics=sem, vmem_limit_bytes=VMEM_LIMIT)


def _pick_tile(n, prefs):
    for t in prefs:
        if n % t == 0:
            return t
    return n


def _mm_kernel(a_ref, w_ref, o_ref):
    o_ref[...] = jnp.dot(a_ref[...].astype(BF16), w_ref[...],
                         preferred_element_type=F32).astype(o_ref.dtype)


def _matmul(a, w, out_dtype):
    m, k = a.shape
    n = w.shape[1]
    tm = _pick_tile(m, (MM_TM, 512, 256, 128))
    tn = _pick_tile(n, (MM_TN, 384, 256, 128))
    return pl.pallas_call(
        _mm_kernel,
        out_shape=jax.ShapeDtypeStruct((m, n), out_dtype),
        grid=(m // tm, n // tn),
        in_specs=[pl.BlockSpec((tm, k), lambda i, j: (i, 0)),
                  pl.BlockSpec((k, tn), lambda i, j: (0, j))],
        out_specs=pl.BlockSpec((tm, tn), lambda i, j: (i, j)),
        compiler_params=_params(("parallel", "parallel")),
        name="matmul",
    )(a, w)


def _split2(x):
    hi = x.astype(BF16)
    lo = (x - hi.astype(F32)).astype(BF16)
    return hi, lo


def _split3(x):
    h1 = x.astype(BF16)
    r1 = x - h1.astype(F32)
    h2 = r1.astype(BF16)
    h3 = (r1 - h2.astype(F32)).astype(BF16)
    return h1, h2, h3


def _log1pexp_neg_abs(z):
    return jnp.log(1.0 + jnp.exp(-jnp.abs(z)))


def _neg_abs(x):
    bits = lax.bitcast_convert_type(x, jnp.uint32) | jnp.uint32(0x80000000)
    return lax.bitcast_convert_type(bits, F32)


def _store_packed_rows(xp_ref, x):
    half = x.shape[1] // 2
    lo = lax.bitcast_convert_type(x[:, :half].astype(BF16).astype(F32), jnp.uint32)
    hi = lax.bitcast_convert_type(x[:, half:].astype(BF16).astype(F32), jnp.uint32)
    xp_ref[...] = (lo >> 16) | (hi & jnp.uint32(0xFFFF0000))


def _unpack_rows(w):
    lo = lax.bitcast_convert_type(w << 16, F32).astype(BF16)
    hi = lax.bitcast_convert_type(w & jnp.uint32(0xFFFF0000), F32).astype(BF16)
    return jnp.concatenate([lo, hi], axis=1)


def _layer_norm(h, g, b):
    mu = jnp.mean(h, axis=-1, keepdims=True)
    c = h - mu
    var = jnp.mean(c * c, axis=-1, keepdims=True)
    return c * lax.rsqrt(var + LN_EPS) * g + b


def _head_cols(hd):
    return slice(hd * HEAD_DIM, (hd + 1) * HEAD_DIM)


def _fox_gate_kernel(x_ref, w_ref, b_ref, o_ref, carry_ref):
    @pl.when(pl.program_id(1) == 0)
    def _():
        carry_ref[...] = jnp.zeros_like(carry_ref)

    f = jnp.dot(x_ref[...], w_ref[...], preferred_element_type=F32) + b_ref[...]
    log_f = jnp.minimum(f, 0.0) - _log1pexp_neg_abs(f)
    ts = log_f.shape[0]
    row = lax.broadcasted_iota(jnp.int32, (ts, ts), 0)
    col = lax.broadcasted_iota(jnp.int32, (ts, ts), 1)
    tri = jnp.where(col <= row, 1.0, 0.0).astype(BF16)
    cum = carry_ref[...]
    for part in _split3(log_f):
        cum = cum + jnp.dot(tri, part, preferred_element_type=F32)
    o_ref[...] = cum
    carry_ref[...] = cum[ts - 1:ts, :]


def _fox_gate(xb, w_f, b_f, batch, seq):
    n, d = xb.shape
    ts = _pick_tile(seq, (GATE_TILE, 256, 128))
    ns = seq // ts
    return pl.pallas_call(
        _fox_gate_kernel,
        out_shape=jax.ShapeDtypeStruct((n, LANES), F32),
        grid=(batch, ns),
        in_specs=[pl.BlockSpec((ts, d), lambda b, j: (b * ns + j, 0)),
                  pl.BlockSpec((d, LANES), lambda b, j: (0, 0)),
                  pl.BlockSpec((1, LANES), lambda b, j: (0, 0))],
        out_specs=pl.BlockSpec((ts, LANES), lambda b, j: (b * ns + j, 0)),
        scratch_shapes=[pltpu.VMEM((1, LANES), F32)],
        compiler_params=_params(("parallel", "arbitrary")),
        name="fox_gate",
    )(xb, w_f, b_f)


def _sb_kernel(q_ref, k_ref, v_ref, u_ref, o_ref, *, tq, ck, group):
    qi = pl.program_id(2)
    n_diag = tq // ck
    n_below = qi * n_diag

    def evaluate(units):
        n = len(units)
        starts = [pl.multiple_of(kb * ck, ck) for _, kb, _, _ in units]
        zs, sps, sums, stricts, laters = [None] * n, [None] * n, [None] * n, [None] * n, [None] * n
        pvs, tots = [None] * n, [None] * n

        def scores(u):
            hd, _, r0, masked = units[u]
            z = lax.dot_general(q_ref[r0:, _head_cols(hd)], k_ref[pl.ds(starts[u], ck), _head_cols(hd)],
                                (((1,), (1,)), ((), ())), preferred_element_type=F32)
            sp = jnp.maximum(z, 0.0) + jnp.log(1.0 + jnp.exp2(_neg_abs(z))) * (1.0 / LN2)
            zs[u], sps[u], sums[u] = z, sp, sp
            if masked:
                q_pos = qi * tq + r0 + lax.broadcasted_iota(jnp.int32, z.shape, 0)
                stricts[u] = (starts[u] + lax.broadcasted_iota(jnp.int32, z.shape, 1)) < q_pos
                sums[u] = jnp.where(stricts[u], sp, 0.0)

        def prefix(u):
            laters[u] = jnp.dot(sums[u].astype(BF16), u_ref[...], preferred_element_type=F32)

        def values(u):
            hd = units[u][0]
            w = jnp.exp2(zs[u] - sps[u] - laters[u])
            if stricts[u] is not None:
                w = jnp.where(stricts[u], w, 0.0)
            pvs[u] = jnp.dot(w.astype(BF16), v_ref[pl.ds(starts[u], ck), _head_cols(hd)],
                             preferred_element_type=F32)
            tots[u] = laters[u][:, :1] + sums[u][:, :1]

        for t in range(n + 2):
            if t >= 2:
                values(t - 2)
            if 1 <= t <= n:
                prefix(t - 1)
            if t < n:
                scores(t)
        return pvs, tots

    def fold(acc, later, pv, tot, r0):
        acc_s = acc[r0:] + pv * jnp.exp2(-later[r0:])
        later_s = later[r0:] + tot
        if r0:
            return (jnp.concatenate([acc[:r0], acc_s], axis=0),
                    jnp.concatenate([later[:r0], later_s], axis=0))
        return acc_s, later_s

    diag_units = [(hd, n_below + d, d * ck, True) for hd in range(group) for d in range(n_diag - 1, -1, -1)]
    pvs, tots = evaluate(diag_units)
    accs, laters = [], []
    for hd in range(group):
        acc, later = jnp.zeros((tq, HEAD_DIM), F32), jnp.zeros((tq, 1), F32)
        for u in range(hd * n_diag, (hd + 1) * n_diag):
            acc, later = fold(acc, later, pvs[u], tots[u], diag_units[u][2])
        accs.append(acc)
        laters.append(later)

    def body(i, carry):
        accs, laters = carry
        kb = n_below - 1 - 2 * i
        pvs, tots = evaluate([(hd, kb - c, 0, False) for hd in range(group) for c in range(2)])
        out = [fold(*fold(accs[hd], laters[hd], pvs[2 * hd], tots[2 * hd], 0),
                    pvs[2 * hd + 1], tots[2 * hd + 1], 0) for hd in range(group)]
        return tuple(o[0] for o in out), tuple(o[1] for o in out)

    accs, _ = lax.fori_loop(0, n_below // 2, body, (tuple(accs), tuple(laters)))
    for hd in range(group):
        o_ref[:, _head_cols(hd)] = accs[hd].astype(o_ref.dtype)


def _sb_attention(h, batch, seq, heads, q_col, k_col, v_col):
    n = h.shape[0]
    tq, ck = min(SB_TQ, seq), min(SB_CK, seq // 2)
    nq = seq // tq
    group = SB_GROUP
    assert (tq // ck) % 2 == 0 and heads % group == 0
    assert q_col % group == 0 and k_col % group == 0 and v_col % group == 0
    gw = group * HEAD_DIM
    j = lax.broadcasted_iota(jnp.int32, (ck, ck), 0)
    s = lax.broadcasted_iota(jnp.int32, (ck, ck), 1)
    u = jnp.where(j > s, 1.0, 0.0).astype(BF16)
    kern = functools.partial(_sb_kernel, tq=tq, ck=ck, group=group)
    return pl.pallas_call(
        kern,
        out_shape=jax.ShapeDtypeStruct((n, heads * HEAD_DIM), BF16),
        grid=(batch, heads // group, nq),
        in_specs=[pl.BlockSpec((tq, gw), lambda b, g, i: (b * nq + i, q_col // group + g)),
                  pl.BlockSpec((seq, gw), lambda b, g, i: (b, k_col // group + g)),
                  pl.BlockSpec((seq, gw), lambda b, g, i: (b, v_col // group + g)),
                  pl.BlockSpec((ck, ck), lambda b, g, i: (0, 0))],
        out_specs=pl.BlockSpec((tq, gw), lambda b, g, i: (b * nq + i, g)),
        compiler_params=_params(("parallel", "parallel", "arbitrary")),
        name="sb_attention",
    )(h, h, h, u)


def _flash_sweep(qs, k_fn, v_fn, bias_fn, qi, tile, dv):
    n = len(qs)

    def sweep_step(kb, states, masked):
        start = pl.multiple_of(kb * tile, tile)
        scores, new = [None] * n, [None] * n

        def score(hd):
            s = lax.dot_general(qs[hd], k_fn(hd, start), (((1,), (1,)), ((), ())), preferred_element_type=F32)
            if bias_fn is not None:
                s = s + bias_fn(hd, kb)
            if masked:
                s = jnp.where(lax.broadcasted_iota(jnp.int32, s.shape, 1)
                              <= lax.broadcasted_iota(jnp.int32, s.shape, 0), s, MASK_VALUE)
            scores[hd] = s

        def update(hd):
            m, l, acc = states[hd]
            s = scores[hd]
            m_new = jnp.maximum(m, jnp.max(s, axis=-1, keepdims=True))
            p = jnp.exp2(s - m_new)
            corr = jnp.exp2(m - m_new)
            l = corr * l + jnp.sum(p, axis=-1, keepdims=True)
            acc = corr * acc + jnp.dot(p.astype(BF16), v_fn(hd, start), preferred_element_type=F32)
            new[hd] = (m_new, l, acc)

        for i in range(n + 1):
            if i < n:
                score(i)
            if i >= 1:
                update(i - 1)
        return tuple(new)

    init = (jnp.full((tile, 1), MASK_VALUE, F32), jnp.zeros((tile, 1), F32), jnp.zeros((tile, dv), F32))
    states = lax.fori_loop(0, qi, lambda kb, st: sweep_step(kb, st, False), tuple(init for _ in range(n)))
    states = sweep_step(qi, states, True)
    return [acc / l for _, l, acc in states]


def _fox_kernel(q_ref, k_ref, v_ref, ck_ref, o_ref, *, tile, group):
    qs = [q_ref[:, _head_cols(hd)] for hd in range(group)]
    outs = _flash_sweep(
        qs,
        lambda hd, start: k_ref[pl.ds(start, tile), _head_cols(hd)],
        lambda hd, start: v_ref[pl.ds(start, tile), _head_cols(hd)],
        lambda hd, kb: ck_ref[hd, pl.ds(kb, 1), :] * (-LOG2E),
        pl.program_id(2), tile, HEAD_DIM)
    for hd in range(group):
        o_ref[:, _head_cols(hd)] = outs[hd].astype(o_ref.dtype)


def _fox_attention(h, cum, batch, seq, heads, q_col, k_col, v_col):
    n = h.shape[0]
    tile = min(FA_TILE, seq)
    nq = seq // tile
    group = FA_GROUP
    assert heads % group == 0 and q_col % group == 0 and k_col % group == 0 and v_col % group == 0
    gw = group * HEAD_DIM
    cum_k = jnp.transpose(cum[:, :heads].reshape(batch, seq, heads), (0, 2, 1)).reshape(batch, heads, nq, tile)
    kern = functools.partial(_fox_kernel, tile=tile, group=group)
    return pl.pallas_call(
        kern,
        out_shape=jax.ShapeDtypeStruct((n, heads * HEAD_DIM), BF16),
        grid=(batch, heads // group, nq),
        in_specs=[pl.BlockSpec((tile, gw), lambda b, g, i: (b * nq + i, q_col // group + g)),
                  pl.BlockSpec((seq, gw), lambda b, g, i: (b, k_col // group + g)),
                  pl.BlockSpec((seq, gw), lambda b, g, i: (b, v_col // group + g)),
                  pl.BlockSpec((None, group, nq, tile), lambda b, g, i: (b, g, 0, 0))],
        out_specs=pl.BlockSpec((tile, gw), lambda b, g, i: (b * nq + i, g)),
        compiler_params=_params(("parallel", "parallel", "arbitrary")),
        name="fox_attention",
    )(h, h, h, cum_k)


def _rope_lanes(t, ra, rb, rc):
    half = ROPE_DIM // 2
    return t * ra + pltpu.roll(t, half, 1) * rb + pltpu.roll(t, LANES - half, 1) * rc


def _mla_kernel(q_ref, kn_ref, kp_ref, v_ref, ra_ref, rb_ref, rc_ref, o_ref, kfull_ref,
                *, tile, group):
    dq = NOPE_DIM + LANES

    @pl.when(pl.program_id(2) == 0)
    def _():
        for hd in range(group):
            kfull_ref[:, hd * dq:hd * dq + NOPE_DIM] = kn_ref[:, _head_cols(hd)]
            kfull_ref[:, hd * dq + NOPE_DIM:(hd + 1) * dq] = kp_ref[...]

    qs = []
    for hd in range(group):
        q = q_ref[:, hd * dq:(hd + 1) * dq]
        q_pe = _rope_lanes(q[:, NOPE_DIM:].astype(F32), ra_ref[...], rb_ref[...], rc_ref[...])
        qs.append(jnp.concatenate([q[:, :NOPE_DIM], q_pe.astype(BF16)], axis=1))
    outs = _flash_sweep(
        qs,
        lambda hd, start: kfull_ref[pl.ds(start, tile), hd * dq:(hd + 1) * dq],
        lambda hd, start: v_ref[pl.ds(start, tile), _head_cols(hd)],
        None, pl.program_id(2), tile, V_DIM)
    for hd in range(group):
        o_ref[:, _head_cols(hd)] = outs[hd].astype(o_ref.dtype)


def _mla_attention(q, kv, k_pe, ra, rb, rc, batch, seq, heads):
    n = q.shape[0]
    tile = min(FA_TILE, seq)
    nq = seq // tile
    group = FA_GROUP
    assert heads % group == 0
    ng = heads // group
    dq = NOPE_DIM + LANES
    kern = functools.partial(_mla_kernel, tile=tile, group=group)
    rope_spec = pl.BlockSpec((tile, LANES), lambda b, g, i: (b * nq + i, 0))
    return pl.pallas_call(
        kern,
        out_shape=jax.ShapeDtypeStruct((n, heads * V_DIM), BF16),
        grid=(batch, ng, nq),
        in_specs=[pl.BlockSpec((tile, group * dq), lambda b, g, i: (b * nq + i, g)),
                  pl.BlockSpec((seq, group * NOPE_DIM), lambda b, g, i: (b, g)),
                  pl.BlockSpec((seq, LANES), lambda b, g, i: (b, 0)),
                  pl.BlockSpec((seq, group * V_DIM), lambda b, g, i: (b, ng + g)),
                  rope_spec, rope_spec, rope_spec],
        out_specs=pl.BlockSpec((tile, group * V_DIM), lambda b, g, i: (b * nq + i, g)),
        scratch_shapes=[pltpu.VMEM((seq, group * dq), BF16)],
        compiler_params=_params(("parallel", "parallel", "arbitrary")),
        name="mla_attention",
    )(q, kv, k_pe, kv, ra, rb, rc)


def _mla_prep_kernel(down_ref, pos_ref, qn_ref, kvn_ref, invf_ref,
                     cq_ref, ckv_ref, kpe_ref, ra_ref, rb_ref, rc_ref):
    down = down_ref[...]

    def rms(c, g):
        return c * lax.rsqrt(jnp.mean(c * c, axis=-1, keepdims=True) + RMS_EPS) * g

    cq_ref[...] = rms(down[:, :Q_LORA], qn_ref[...]).astype(cq_ref.dtype)
    ckv_ref[...] = rms(down[:, Q_LORA:Q_LORA + KV_LORA], kvn_ref[...]).astype(ckv_ref.dtype)

    ang = pos_ref[...].astype(F32) * invf_ref[...]
    cos, sin = jnp.cos(ang), jnp.sin(ang)
    lane = lax.broadcasted_iota(jnp.int32, ang.shape, 1)
    half = ROPE_DIM // 2
    ra = jnp.where(lane < ROPE_DIM, cos, 0.0)
    rb = jnp.where((lane >= half) & (lane < ROPE_DIM), sin, 0.0)
    rc = jnp.where(lane < half, -sin, 0.0)
    ra_ref[...] = ra
    rb_ref[...] = rb
    rc_ref[...] = rc
    kpe_ref[...] = _rope_lanes(down[:, Q_LORA + KV_LORA:], ra, rb, rc).astype(kpe_ref.dtype)


def _mla_prep(down, pos, q_norm, kv_norm, inv_freq):
    n, dd = down.shape
    tm = _pick_tile(n, (ROW_TILE, 128))
    row = lambda w: pl.BlockSpec((tm, w), lambda i: (i, 0))
    const = lambda w: pl.BlockSpec((1, w), lambda i: (0, 0))
    return pl.pallas_call(
        _mla_prep_kernel,
        out_shape=(jax.ShapeDtypeStruct((n, Q_LORA), BF16),
                   jax.ShapeDtypeStruct((n, KV_LORA), BF16),
                   jax.ShapeDtypeStruct((n, LANES), BF16),
                   jax.ShapeDtypeStruct((n, LANES), F32),
                   jax.ShapeDtypeStruct((n, LANES), F32),
                   jax.ShapeDtypeStruct((n, LANES), F32)),
        grid=(n // tm,),
        in_specs=[row(dd), row(1), const(Q_LORA), const(KV_LORA), const(LANES)],
        out_specs=(row(Q_LORA), row(KV_LORA), row(LANES), row(LANES), row(LANES), row(LANES)),
        compiler_params=_params(("parallel",)),
        name="mla_prep",
    )(down, pos, q_norm, kv_norm, inv_freq)


def _ln_router_kernel(x_ref, mix_ref, g_ref, b_ref, rw_hi_ref, rw_lo_ref, rb_ref,
                      x1_ref, xp_ref, e_ref, gate_ref, rank_ref, cnt_ref, carry_ref, *, alpha):
    @pl.when(pl.program_id(0) == 0)
    def _():
        carry_ref[...] = jnp.zeros_like(carry_ref)

    x1 = _layer_norm(alpha * x_ref[...] + mix_ref[...], g_ref[...], b_ref[...])
    x1_ref[...] = x1
    _store_packed_rows(xp_ref, x1)

    a_hi, a_lo = _split2(x1)
    rw_hi, rw_lo = rw_hi_ref[...], rw_lo_ref[...]
    logits = (jnp.dot(a_hi, rw_hi, preferred_element_type=F32)
              + jnp.dot(a_hi, rw_lo, preferred_element_type=F32)
              + jnp.dot(a_lo, rw_hi, preferred_element_type=F32)) + rb_ref[...]

    tm = logits.shape[0]
    lane = lax.broadcasted_iota(jnp.int32, (tm, LANES), 1)
    work = logits
    picks, vals = [], []
    for _ in range(TOP_K):
        mx = jnp.max(work, axis=-1, keepdims=True)
        idx = jnp.min(jnp.where(work == mx, lane, LANES), axis=-1, keepdims=True)
        pick = lane == idx
        picks.append(pick)
        vals.append(mx)
        work = jnp.where(pick, -jnp.inf, work)

    exps = [jnp.exp(v - vals[0]) for v in vals]
    denom = exps[0]
    for e in exps[1:]:
        denom = denom + e

    chosen = jnp.zeros((tm, LANES), F32)
    for pick in picks:
        chosen = chosen + jnp.where(pick, 1.0, 0.0)
    row = lax.broadcasted_iota(jnp.int32, (tm, tm), 0)
    col = lax.broadcasted_iota(jnp.int32, (tm, tm), 1)
    strict_lower = jnp.where(col < row, 1.0, 0.0).astype(BF16)
    before = jnp.dot(strict_lower, chosen.astype(BF16), preferred_element_type=F32) + carry_ref[...]

    e_out = jnp.zeros((tm, LANES), jnp.int32)
    gate_out = jnp.zeros((tm, LANES), F32)
    rank_out = jnp.zeros((tm, LANES), F32)
    for k in range(TOP_K):
        idx = jnp.min(jnp.where(picks[k], lane, LANES), axis=-1, keepdims=True)
        rank = jnp.sum(jnp.where(picks[k], before, 0.0), axis=-1, keepdims=True)
        e_out = jnp.where(lane == k, idx, e_out)
        gate_out = jnp.where(lane == k, exps[k] / denom, gate_out)
        rank_out = jnp.where(lane == k, rank, rank_out)
    e_ref[...] = e_out
    gate_ref[...] = gate_out
    rank_ref[...] = rank_out.astype(jnp.int32)

    carry_ref[...] = carry_ref[...] + jnp.sum(chosen, axis=0, keepdims=True)
    cnt_ref[...] = carry_ref[...]


def _ln_router(x, mix, g, b, rw_hi, rw_lo, rb, alpha):
    n, d = x.shape
    tm = _pick_tile(n, (ROW_TILE, 128))
    row = lambda w: pl.BlockSpec((tm, w), lambda i: (i, 0))
    const = lambda r, w: pl.BlockSpec((r, w), lambda i: (0, 0))
    return pl.pallas_call(
        functools.partial(_ln_router_kernel, alpha=alpha),
        out_shape=(jax.ShapeDtypeStruct((n, d), F32),
                   jax.ShapeDtypeStruct((n, d // 2), jnp.uint32),
                   jax.ShapeDtypeStruct((n, LANES), jnp.int32),
                   jax.ShapeDtypeStruct((n, LANES), F32),
                   jax.ShapeDtypeStruct((n, LANES), jnp.int32),
                   jax.ShapeDtypeStruct((1, LANES), F32)),
        grid=(n // tm,),
        in_specs=[row(d), row(d), const(1, d), const(1, d), const(d, LANES), const(d, LANES),
                  const(1, LANES)],
        out_specs=(row(d), row(d // 2), row(LANES), row(LANES), row(LANES), const(1, LANES)),
        scratch_shapes=[pltpu.VMEM((1, LANES), F32)],
        compiler_params=_params(("arbitrary",)),
        name="ln_router",
    )(x, mix, g, b, rw_hi, rw_lo, rb)


SC_CORES = 2
SC_SUBCORES = 16
SC_CHUNK = 16
SC_RING = 4
EXPERT_STAGES = 4


def _sc_gather_rows(table, idx):
    b = idx.shape[0]
    width = table.shape[1]
    workers = SC_CORES * SC_SUBCORES
    per_worker = b // workers
    assert b % (workers * SC_CHUNK) == 0
    n_chunks = per_worker // SC_CHUNK
    assert n_chunks >= SC_RING
    mesh = plsc.VectorSubcoreMesh(core_axis_name="c", subcore_axis_name="s",
                                  num_cores=SC_CORES, num_subcores=SC_SUBCORES)

    @functools.partial(
        pl.kernel, mesh=mesh,
        out_type=jax.ShapeDtypeStruct((b, width), table.dtype),
        scratch_types=[pltpu.VMEM((per_worker,), jnp.int32),
                       pltpu.VMEM((SC_RING, SC_CHUNK, width), table.dtype),
                       pltpu.SemaphoreType.DMA((SC_RING,)),
                       pltpu.SemaphoreType.DMA((SC_RING,))],
        name="sc_gather_rows",
    )
    def gather(table_hbm, idx_hbm, out_hbm, idx_v, rows_v, gsem, wsem):
        wid = lax.axis_index("s") * SC_CORES + lax.axis_index("c")
        base = wid * per_worker
        pltpu.sync_copy(idx_hbm.at[pl.ds(base, per_worker)], idx_v)

        def gather_start(j, slot):
            pltpu.async_copy(table_hbm.at[idx_v.at[pl.ds(j * SC_CHUNK, SC_CHUNK)]], rows_v.at[slot], gsem.at[slot])

        def gather_wait(slot):
            pltpu.make_async_copy(table_hbm.at[pl.ds(0, SC_CHUNK)], rows_v.at[slot], gsem.at[slot]).wait()

        def write_copy(j, slot):
            return pltpu.make_async_copy(rows_v.at[slot], out_hbm.at[pl.ds(base + j * SC_CHUNK, SC_CHUNK)],
                                         wsem.at[slot])

        for s in range(SC_RING - 1):
            gather_start(s, s)

        def body(j, _):
            slot = j % SC_RING
            gather_wait(slot)
            write_copy(j, slot).start()
            nxt = j + SC_RING - 1
            nslot = nxt % SC_RING

            @pl.when(nxt < n_chunks)
            def _():
                @pl.when(j >= 1)
                def _():
                    write_copy(j - 1, nslot).wait()
                gather_start(nxt, nslot)
            return 0

        lax.fori_loop(0, n_chunks, body, 0)
        for s in range(SC_RING):
            j = n_chunks - SC_RING + s
            write_copy(j, j % SC_RING).wait()

    return gather(table, idx)


def _expert_kernel(blk_e_ref, n_used_ref, xs_ref, wgu_ref, bgu_ref, wdn_ref, bdn_ref, *rest,
                   first_blk, d_expert):
    y_ref = rest[-1]
    used = first_blk + pl.program_id(0) < n_used_ref[0]

    @pl.when(used)
    def _():
        xb = _unpack_rows(xs_ref[...])
        hb = jnp.dot(xb, wgu_ref[...], preferred_element_type=F32) + bgu_ref[...]
        g = jnp.minimum(hb[:, :d_expert], SWIGLU_LIMIT)
        u = jnp.clip(hb[:, d_expert:], -SWIGLU_LIMIT, SWIGLU_LIMIT)
        act = (u + 1.0) * (g * (1.0 / (1.0 + jnp.exp(-SWIGLU_ALPHA * g))))
        y_ref[...] = jnp.dot(act.astype(BF16), wdn_ref[...], preferred_element_type=F32) + bdn_ref[...]

    @pl.when(jnp.logical_not(used))
    def _():
        y_ref[...] = jnp.zeros_like(y_ref)


def _experts_stage(xs, y, n_rows, blk_e, n_used, w_gu, b_gu, w_dn, b_dn, layer, bm, first_blk):
    _, _, d, f2 = w_gu.shape
    d_expert = f2 // 2
    stage_blk = xs.shape[0] // bm
    carried = [] if y is None else [y]
    wmap = lambda i, be, nu: (layer, be[first_blk + i], 0, 0)
    xmap = lambda i, be, nu: (jnp.minimum(i, jnp.maximum(nu[0] - first_blk - 1, 0)), 0)
    grid_spec = pltpu.PrefetchScalarGridSpec(
        num_scalar_prefetch=2,
        grid=(stage_blk,),
        in_specs=[pl.BlockSpec((bm, d // 2), xmap),
                  pl.BlockSpec((None, None, d, f2), wmap),
                  pl.BlockSpec((None, None, 1, f2), wmap),
                  pl.BlockSpec((None, None, d_expert, d), wmap),
                  pl.BlockSpec((None, None, 1, d), wmap)]
                 + [pl.BlockSpec(memory_space=pl.ANY) for _ in carried],
        out_specs=pl.BlockSpec((bm, d), lambda i, be, nu: (first_blk + i, 0)),
    )
    return pl.pallas_call(
        functools.partial(_expert_kernel, first_blk=first_blk, d_expert=d_expert),
        out_shape=jax.ShapeDtypeStruct((n_rows, d), F32),
        grid_spec=grid_spec,
        input_output_aliases={7: 0} if carried else {},
        compiler_params=_params(("arbitrary",)),
        name="experts",
    )(blk_e, n_used, xs, w_gu, b_gu[:, :, None, :], w_dn, b_dn[:, :, None, :], *carried)


def _experts(xp, blk_e, n_used, row_tok, w_gu, b_gu, w_dn, b_dn, layer, bm):
    n_rows = row_tok.shape[0]
    stage_rows = n_rows // EXPERT_STAGES
    table = lax.bitcast_convert_type(xp, jnp.int32)
    y = None
    for s in range(EXPERT_STAGES):
        xs = _sc_gather_rows(table, row_tok[s * stage_rows:(s + 1) * stage_rows])
        y = _experts_stage(lax.bitcast_convert_type(xs, jnp.uint32), y, n_rows, blk_e, n_used,
                           w_gu, b_gu, w_dn, b_dn, layer, bm, s * (stage_rows // bm))
    return y


def _combine_kernel(pos_ref, y_hbm, x1_ref, gate_ref, g_ref, b_ref, x2_ref, xb_ref, ybuf, sem, *, tm, alpha):
    i = pl.program_id(0)
    n_steps = pl.num_programs(0)

    def gather_copy(src_row, k, t, slot):
        return pltpu.make_async_copy(y_hbm.at[pl.ds(src_row, 1)], ybuf.at[slot, k, pl.ds(t, 1)], sem.at[slot])

    def issue(step, slot):
        def body(t, _):
            for k in range(TOP_K):
                gather_copy(pos_ref[(step * tm + t) * TOP_K + k], k, t, slot).start()
            return 0
        lax.fori_loop(0, tm, body, 0, unroll=2)

    @pl.when(i == 0)
    def _():
        issue(0, 0)

    @pl.when(i + 1 < n_steps)
    def _():
        issue(i + 1, (i + 1) % 2)

    slot = i % 2

    def wait_body(t, _):
        for k in range(TOP_K):
            gather_copy(0, k, t, slot).wait()
        return 0
    lax.fori_loop(0, tm, wait_body, 0, unroll=2)

    gate = gate_ref[...]
    ffn = gate[:, 0:1] * ybuf[slot, 0]
    for k in range(1, TOP_K):
        ffn = ffn + gate[:, k:k + 1] * ybuf[slot, k]
    x2 = _layer_norm(alpha * x1_ref[...] + ffn, g_ref[...], b_ref[...])
    x2_ref[...] = x2
    xb_ref[...] = x2.astype(BF16)


def _combine_ln(pos_flat, y, x1, gate, g, b, alpha):
    n, d = x1.shape
    tm = _pick_tile(n, (COMBINE_TM,))
    grid_spec = pltpu.PrefetchScalarGridSpec(
        num_scalar_prefetch=1,
        grid=(n // tm,),
        in_specs=[pl.BlockSpec(memory_space=pl.ANY),
                  pl.BlockSpec((tm, d), lambda i, p: (i, 0)),
                  pl.BlockSpec((tm, LANES), lambda i, p: (i, 0)),
                  pl.BlockSpec((1, d), lambda i, p: (0, 0)),
                  pl.BlockSpec((1, d), lambda i, p: (0, 0))],
        out_specs=(pl.BlockSpec((tm, d), lambda i, p: (i, 0)),
                   pl.BlockSpec((tm, d), lambda i, p: (i, 0))),
        scratch_shapes=[pltpu.VMEM((2, TOP_K, tm, d), F32), pltpu.SemaphoreType.DMA((2,))],
    )
    return pl.pallas_call(
        functools.partial(_combine_kernel, tm=tm, alpha=alpha),
        out_shape=(jax.ShapeDtypeStruct((n, d), F32), jax.ShapeDtypeStruct((n, d), BF16)),
        grid_spec=grid_spec,
        compiler_params=_params(("arbitrary",)),
        name="combine_ln",
    )(pos_flat, y, x1, gate, g, b)


def _pad_cols(w, width):
    return jnp.pad(w, ((0, 0), (0, width - w.shape[1])))


def _moe_layer(x, mix, ln_g, ln_b, router_w, router_b, w_gu, b_gu, w_dn, b_dn, layer, ln2_g, ln2_b, alpha):
    n, d = x.shape
    n_exp = router_w.shape[1]
    bm = EXPERT_BM

    rw = _pad_cols(router_w, LANES)
    rw_hi = rw.astype(BF16)
    rw_lo = (rw - rw_hi.astype(F32)).astype(BF16)
    rb = jnp.full((1, LANES), MASK_VALUE, F32).at[0, :n_exp].set(router_b)
    x1, xp, top_e, gate, rank, cnt = _ln_router(x, mix, ln_g[None], ln_b[None], rw_hi, rw_lo, rb, alpha)

    counts = cnt[0, :n_exp].astype(jnp.int32)
    padded = (counts + bm - 1) // bm * bm
    pend = jnp.cumsum(padded)
    pstart = pend - padded
    top_e = top_e[:, :TOP_K]
    pos = (pstart[top_e] + rank[:, :TOP_K]).reshape(-1)
    n_blk = -(-(n * TOP_K + n_exp * (bm - 1)) // bm)
    n_blk = -(-n_blk // EXPERT_STAGES) * EXPERT_STAGES
    n_rows = n_blk * bm
    tok = jnp.repeat(jnp.arange(n, dtype=jnp.int32), TOP_K)
    row_tok = jnp.zeros((n_rows,), jnp.int32).at[pos].set(tok)
    blk_start = jnp.arange(n_blk, dtype=jnp.int32) * bm
    blk_e = jnp.minimum(jnp.sum((pend[None, :] <= blk_start[:, None]).astype(jnp.int32), axis=1), n_exp - 1)
    n_used = (pend[-1:] // bm).astype(jnp.int32)

    y = _experts(xp, blk_e, n_used, row_tok, w_gu, b_gu, w_dn, b_dn, layer, bm)
    return _combine_ln(pos, y, x1, gate, ln2_g[None], ln2_b[None], alpha)


def _even_mixer(xb, w_in, b_f, w_o, batch, seq):
    sbw = SB_HEADS * HEAD_DIM
    fxw = FOX_HEADS * HEAD_DIM
    main = 3 * sbw + 3 * fxw
    zscale = HEAD_DIM ** -0.5 * LOG2E
    col = jnp.arange(main)
    is_q = (col < sbw) | ((col >= 3 * sbw) & (col < 3 * sbw + fxw))
    h = _matmul(xb, (w_in[:, :main] * jnp.where(is_q, zscale, 1.0)).astype(BF16), BF16)
    w_f = _pad_cols(w_in[:, main:], LANES).astype(BF16)
    cum = _fox_gate(xb, w_f, _pad_cols(b_f[None], LANES), batch, seq)
    o_sb = _sb_attention(h, batch, seq, SB_HEADS, 0, SB_HEADS, 2 * SB_HEADS)
    c0 = 3 * SB_HEADS
    o_fx = _fox_attention(h, cum, batch, seq, FOX_HEADS, c0, c0 + FOX_HEADS, c0 + 2 * FOX_HEADS)
    o = jnp.concatenate([o_sb, o_fx], axis=-1)
    return _matmul(o, w_o.astype(BF16), F32)


def _mla_mixer(xb, pos, w_down, q_norm, kv_norm, w_uq, w_ukv, w_o, batch, seq):
    hds = MLA_HEADS
    w_down_p = _pad_cols(w_down, Q_LORA + KV_LORA + LANES).astype(BF16)
    down = _matmul(xb, w_down_p, F32)
    freq = ROPE_THETA ** (-jnp.arange(0, ROPE_DIM, 2, dtype=F32) / ROPE_DIM)
    inv_freq = jnp.concatenate([freq, freq, jnp.zeros((LANES - ROPE_DIM,), F32)])[None]
    cq, ckv, k_pe, ra, rb, rc = _mla_prep(down, pos, q_norm[None], kv_norm[None], inv_freq)

    wq = w_uq.reshape(Q_LORA, hds, NOPE_DIM + ROPE_DIM)
    wq = jnp.pad(wq, ((0, 0), (0, 0), (0, LANES - ROPE_DIM))).reshape(Q_LORA, hds * (NOPE_DIM + LANES))
    wkv = w_ukv.reshape(KV_LORA, hds, NOPE_DIM + V_DIM)
    wkv = jnp.concatenate([wkv[:, :, :NOPE_DIM].reshape(KV_LORA, hds * NOPE_DIM),
                           wkv[:, :, NOPE_DIM:].reshape(KV_LORA, hds * V_DIM)], axis=1)
    q = _matmul(cq, (wq * ((NOPE_DIM + ROPE_DIM) ** -0.5 * LOG2E)).astype(BF16), BF16)
    kv = _matmul(ckv, wkv.astype(BF16), BF16)
    o = _mla_attention(q, kv, k_pe, ra, rb, rc, batch, seq, hds)
    return _matmul(o, w_o.astype(BF16), F32)


def kernel(x, positions, ln_mix_g, ln_mix_b, ln_ffn_g, ln_ffn_b, even_w_in, fox_b_f, even_w_o, mla_w_down, mla_q_norm, mla_kv_norm, mla_w_uq, mla_w_ukv, mla_w_o, router_w, router_b, expert_w_gate_up, expert_b_gate_up, expert_w_down, expert_b_down):
    batch, seq, d = x.shape
    depth = ln_mix_g.shape[0]
    alpha = (2 * depth) ** 0.25
    n = batch * seq
    xf = x.reshape(n, d)
    xb = xf.astype(BF16)
    pos = positions.reshape(n, 1)
    w_gu = expert_w_gate_up.astype(BF16)
    w_dn = expert_w_down.astype(BF16)
    for layer in range(depth):
        i = layer // 2
        if layer % 2 == 0:
            mix = _even_mixer(xb, even_w_in[i], fox_b_f[i], even_w_o[i], batch, seq)
        else:
            mix = _mla_mixer(xb, pos, mla_w_down[i], mla_q_norm[i], mla_kv_norm[i],
                             mla_w_uq[i], mla_w_ukv[i], mla_w_o[i], batch, seq)
        xf, xb = _moe_layer(xf, mix, ln_mix_g[layer], ln_mix_b[layer], router_w[layer], router_b[layer],
                            w_gu, expert_b_gate_up, w_dn, expert_b_down, layer,
                            ln_ffn_g[layer], ln_ffn_b[layer], alpha)
    return xf.reshape(batch, seq, d)
```

```python
import functools

import jax
import jax.numpy as jnp
from jax import lax
from jax.experimental import pallas as pl
from jax.experimental.pallas import tpu as pltpu

F32 = jnp.float32
BF16 = jnp.bfloat16

HEAD_DIM = 128
SB_HEADS = 8
FOX_HEADS = 8
MLA_HEADS = 16
Q_LORA = 512
KV_LORA = 512
NOPE_DIM = 128
ROPE_DIM = 64
V_DIM = 128
ROPE_THETA = 10000.0
N_EXPERTS = 32
TOP_K = 4
SWIGLU_LIMIT = 7.0
SWIGLU_ALPHA = 1.702
LN_EPS = 1e-5
RMS_EPS = 1e-6

LANES = 128
MASK_VALUE = -1e30
VMEM_LIMIT = 56 * 1024 * 1024
LOG2E = 1.4426950408889634
LN2 = 0.6931471805599453

MM_TM = 1024
MM_TN = 512
SB_TQ = 512
SB_CK = 256
SB_GROUP = 4
FA_TILE = 512
FA_GROUP = 4
ROW_TILE = 256
GATE_TILE = 512
EXPERT_BM = 512
COMBINE_TM = 128


def _params(sem):
    return pltpu.CompilerParams(dimension_semantics=sem, vmem_limit_bytes=VMEM_LIMIT)


def _pick_tile(n, prefs):
    for t in prefs:
        if n % t == 0:
            return t
    return n


def _mm_kernel(a_ref, w_ref, o_ref):
    o_ref[...] = jnp.dot(a_ref[...].astype(BF16), w_ref[...],
                         preferred_element_type=F32).astype(o_ref.dtype)


def _matmul(a, w, out_dtype):
    m, k = a.shape
    n = w.shape[1]
    tm = _pick_tile(m, (MM_TM, 512, 256, 128))
    tn = _pick_tile(n, (MM_TN, 384, 256, 128))
    return pl.pallas_call(
        _mm_kernel,
        out_shape=jax.ShapeDtypeStruct((m, n), out_dtype),
        grid=(m // tm, n // tn),
        in_specs=[pl.BlockSpec((tm, k), lambda i, j: (i, 0)),
                  pl.BlockSpec((k, tn), lambda i, j: (0, j))],
        out_specs=pl.BlockSpec((tm, tn), lambda i, j: (i, j)),
        compiler_params=_params(("parallel", "parallel")),
        name="matmul",
    )(a, w)


def _split2(x):
    hi = x.astype(BF16)
    lo = (x - hi.astype(F32)).astype(BF16)
    return hi, lo


def _split3(x):
    h1 = x.astype(BF16)
    r1 = x - h1.astype(F32)
    h2 = r1.astype(BF16)
    h3 = (r1 - h2.astype(F32)).astype(BF16)
    return h1, h2, h3


def _log1pexp_neg_abs(z):
    return jnp.log(1.0 + jnp.exp(-jnp.abs(z)))


def _neg_abs(x):
    bits = lax.bitcast_convert_type(x, jnp.uint32) | jnp.uint32(0x80000000)
    return lax.bitcast_convert_type(bits, F32)


def _store_packed_rows(xp_ref, x):
    half = x.shape[1] // 2
    lo = lax.bitcast_convert_type(x[:, :half].astype(BF16).astype(F32), jnp.uint32)
    hi = lax.bitcast_convert_type(x[:, half:].astype(BF16).astype(F32), jnp.uint32)
    xp_ref[...] = (lo >> 16) | (hi & jnp.uint32(0xFFFF0000))


def _unpack_rows(w):
    lo = lax.bitcast_convert_type(w << 16, F32).astype(BF16)
    hi = lax.bitcast_convert_type(w & jnp.uint32(0xFFFF0000), F32).astype(BF16)
    return jnp.concatenate([lo, hi], axis=1)


def _layer_norm(h, g, b):
    mu = jnp.mean(h, axis=-1, keepdims=True)
    c = h - mu
    var = jnp.mean(c * c, axis=-1, keepdims=True)
    return c * lax.rsqrt(var + LN_EPS) * g + b


def _head_cols(hd):
    return slice(hd * HEAD_DIM, (hd + 1) * HEAD_DIM)


def _fox_gate_kernel(x_ref, w_ref, b_ref, o_ref, carry_ref):
    @pl.when(pl.program_id(1) == 0)
    def _():
        carry_ref[...] = jnp.zeros_like(carry_ref)

    f = jnp.dot(x_ref[...], w_ref[...], preferred_element_type=F32) + b_ref[...]
    log_f = jnp.minimum(f, 0.0) - _log1pexp_neg_abs(f)
    ts = log_f.shape[0]
    row = lax.broadcasted_iota(jnp.int32, (ts, ts), 0)
    col = lax.broadcasted_iota(jnp.int32, (ts, ts), 1)
    tri = jnp.where(col <= row, 1.0, 0.0).astype(BF16)
    cum = carry_ref[...]
    for part in _split3(log_f):
        cum = cum + jnp.dot(tri, part, preferred_element_type=F32)
    o_ref[...] = cum
    carry_ref[...] = cum[ts - 1:ts, :]


def _fox_gate(xb, w_f, b_f, batch, seq):
    n, d = xb.shape
    ts = _pick_tile(seq, (GATE_TILE, 256, 128))
    ns = seq // ts
    return pl.pallas_call(
        _fox_gate_kernel,
        out_shape=jax.ShapeDtypeStruct((n, LANES), F32),
        grid=(batch, ns),
        in_specs=[pl.BlockSpec((ts, d), lambda b, j: (b * ns + j, 0)),
                  pl.BlockSpec((d, LANES), lambda b, j: (0, 0)),
                  pl.BlockSpec((1, LANES), lambda b, j: (0, 0))],
        out_specs=pl.BlockSpec((ts, LANES), lambda b, j: (b * ns + j, 0)),
        scratch_shapes=[pltpu.VMEM((1, LANES), F32)],
        compiler_params=_params(("parallel", "arbitrary")),
        name="fox_gate",
    )(xb, w_f, b_f)


def _sb_kernel(q_ref, k_ref, v_ref, u_ref, o_ref, *, tq, ck, group):
    qi = pl.program_id(2)
    n_diag = tq // ck
    n_below = qi * n_diag

    def evaluate(units):
        n = len(units)
        starts = [pl.multiple_of(kb * ck, ck) for _, kb, _, _ in units]
        zs, sps, sums, stricts, laters = [None] * n, [None] * n, [None] * n, [None] * n, [None] * n
        pvs, tots = [None] * n, [None] * n

        def scores(u):
            hd, _, r0, masked = units[u]
            z = lax.dot_general(q_ref[r0:, _head_cols(hd)], k_ref[pl.ds(starts[u], ck), _head_cols(hd)],
                                (((1,), (1,)), ((), ())), preferred_element_type=F32)
            sp = jnp.maximum(z, 0.0) + jnp.log(1.0 + jnp.exp2(_neg_abs(z))) * (1.0 / LN2)
            zs[u], sps[u], sums[u] = z, sp, sp
            if masked:
                q_pos = qi * tq + r0 + lax.broadcasted_iota(jnp.int32, z.shape, 0)
                stricts[u] = (starts[u] + lax.broadcasted_iota(jnp.int32, z.shape, 1)) < q_pos
                sums[u] = jnp.where(stricts[u], sp, 0.0)

        def prefix(u):
            laters[u] = jnp.dot(sums[u].astype(BF16), u_ref[...], preferred_element_type=F32)

        def values(u):
            hd = units[u][0]
            w = jnp.exp2(zs[u] - sps[u] - laters[u])
            if stricts[u] is not None:
                w = jnp.where(stricts[u], w, 0.0)
            pvs[u] = jnp.dot(w.astype(BF16), v_ref[pl.ds(starts[u], ck), _head_cols(hd)],
                             preferred_element_type=F32)
            tots[u] = laters[u][:, :1] + sums[u][:, :1]

        for t in range(n + 2):
            if t >= 2:
                values(t - 2)
            if 1 <= t <= n:
                prefix(t - 1)
            if t < n:
                scores(t)
        return pvs, tots

    def fold(acc, later, pv, tot, r0):
        acc_s = acc[r0:] + pv * jnp.exp2(-later[r0:])
        later_s = later[r0:] + tot
        if r0:
            return (jnp.concatenate([acc[:r0], acc_s], axis=0),
                    jnp.concatenate([later[:r0], later_s], axis=0))
        return acc_s, later_s

    diag_units = [(hd, n_below + d, d * ck, True) for hd in range(group) for d in range(n_diag - 1, -1, -1)]
    pvs, tots = evaluate(diag_units)
    accs, laters = [], []
    for hd in range(group):
        acc, later = jnp.zeros((tq, HEAD_DIM), F32), jnp.zeros((tq, 1), F32)
        for u in range(hd * n_diag, (hd + 1) * n_diag):
            acc, later = fold(acc, later, pvs[u], tots[u], diag_units[u][2])
        accs.append(acc)
        laters.append(later)

    def body(i, carry):
        accs, laters = carry
        kb = n_below - 1 - 2 * i
        pvs, tots = evaluate([(hd, kb - c, 0, False) for hd in range(group) for c in range(2)])
        out = [fold(*fold(accs[hd], laters[hd], pvs[2 * hd], tots[2 * hd], 0),
                    pvs[2 * hd + 1], tots[2 * hd + 1], 0) for hd in range(group)]
        return tuple(o[0] for o in out), tuple(o[1] for o in out)

    accs, _ = lax.fori_loop(0, n_below // 2, body, (tuple(accs), tuple(laters)))
    for hd in range(group):
        o_ref[:, _head_cols(hd)] = accs[hd].astype(o_ref.dtype)


def _sb_attention(h, batch, seq, heads, q_col, k_col, v_col):
    n = h.shape[0]
    tq, ck = min(SB_TQ, seq), min(SB_CK, seq // 2)
    nq = seq // tq
    group = SB_GROUP
    assert (tq // ck) % 2 == 0 and heads % group == 0
    assert q_col % group == 0 and k_col % group == 0 and v_col % group == 0
    gw = group * HEAD_DIM
    j = lax.broadcasted_iota(jnp.int32, (ck, ck), 0)
    s = lax.broadcasted_iota(jnp.int32, (ck, ck), 1)
    u = jnp.where(j > s, 1.0, 0.0).astype(BF16)
    kern = functools.partial(_sb_kernel, tq=tq, ck=ck, group=group)
    return pl.pallas_call(
        kern,
        out_shape=jax.ShapeDtypeStruct((n, heads * HEAD_DIM), BF16),
        grid=(batch, heads // group, nq),
        in_specs=[pl.BlockSpec((tq, gw), lambda b, g, i: (b * nq + i, q_col // group + g)),
                  pl.BlockSpec((seq, gw), lambda b, g, i: (b, k_col // group + g)),
                  pl.BlockSpec((seq, gw), lambda b, g, i: (b, v_col // group + g)),
                  pl.BlockSpec((ck, ck), lambda b, g, i: (0, 0))],
        out_specs=pl.BlockSpec((tq, gw), lambda b, g, i: (b * nq + i, g)),
        compiler_params=_params(("parallel", "parallel", "arbitrary")),
        name="sb_attention",
    )(h, h, h, u)


def _flash_sweep(qs, k_fn, v_fn, bias_fn, qi, tile, dv):
    n = len(qs)

    def sweep_step(kb, states, masked):
        start = pl.multiple_of(kb * tile, tile)
        scores, new = [None] * n, [None] * n

        def score(hd):
            s = lax.dot_general(qs[hd], k_fn(hd, start), (((1,), (1,)), ((), ())), preferred_element_type=F32)
            if bias_fn is not None:
                s = s + bias_fn(hd, kb)
            if masked:
                s = jnp.where(lax.broadcasted_iota(jnp.int32, s.shape, 1)
                              <= lax.broadcasted_iota(jnp.int32, s.shape, 0), s, MASK_VALUE)
            scores[hd] = s

        def update(hd):
            m, l, acc = states[hd]
            s = scores[hd]
            m_new = jnp.maximum(m, jnp.max(s, axis=-1, keepdims=True))
            p = jnp.exp2(s - m_new)
            corr = jnp.exp2(m - m_new)
            l = corr * l + jnp.sum(p, axis=-1, keepdims=True)
            acc = corr * acc + jnp.dot(p.astype(BF16), v_fn(hd, start), preferred_element_type=F32)
            new[hd] = (m_new, l, acc)

        for i in range(n + 1):
            if i < n:
                score(i)
            if i >= 1:
                update(i - 1)
        return tuple(new)

    init = (jnp.full((tile, 1), MASK_VALUE, F32), jnp.zeros((tile, 1), F32), jnp.zeros((tile, dv), F32))
    states = lax.fori_loop(0, qi, lambda kb, st: sweep_step(kb, st, False), tuple(init for _ in range(n)))
    states = sweep_step(qi, states, True)
    return [acc / l for _, l, acc in states]


def _fox_kernel(q_ref, k_ref, v_ref, ck_ref, o_ref, *, tile, group):
    qs = [q_ref[:, _head_cols(hd)] for hd in range(group)]
    outs = _flash_sweep(
        qs,
        lambda hd, start: k_ref[pl.ds(start, tile), _head_cols(hd)],
        lambda hd, start: v_ref[pl.ds(start, tile), _head_cols(hd)],
        lambda hd, kb: ck_ref[hd, pl.ds(kb, 1), :] * (-LOG2E),
        pl.program_id(2), tile, HEAD_DIM)
    for hd in range(group):
        o_ref[:, _head_cols(hd)] = outs[hd].astype(o_ref.dtype)


def _fox_attention(h, cum, batch, seq, heads, q_col, k_col, v_col):
    n = h.shape[0]
    tile = min(FA_TILE, seq)
    nq = seq // tile
    group = FA_GROUP
    assert heads % group == 0 and q_col % group == 0 and k_col % group == 0 and v_col % group == 0
    gw = group * HEAD_DIM
    cum_k = jnp.transpose(cum[:, :heads].reshape(batch, seq, heads), (0, 2, 1)).reshape(batch, heads, nq, tile)
    kern = functools.partial(_fox_kernel, tile=tile, group=group)
    return pl.pallas_call(
        kern,
        out_shape=jax.ShapeDtypeStruct((n, heads * HEAD_DIM), BF16),
        grid=(batch, heads // group, nq),
        in_specs=[pl.BlockSpec((tile, gw), lambda b, g, i: (b * nq + i, q_col // group + g)),
                  pl.BlockSpec((seq, gw), lambda b, g, i: (b, k_col // group + g)),
                  pl.BlockSpec((seq, gw), lambda b, g, i: (b, v_col // group + g)),
                  pl.BlockSpec((None, group, nq, tile), lambda b, g, i: (b, g, 0, 0))],
        out_specs=pl.BlockSpec((tile, gw), lambda b, g, i: (b * nq + i, g)),
        compiler_params=_params(("parallel", "parallel", "arbitrary")),
        name="fox_attention",
    )(h, h, h, cum_k)


def _rope_lanes(t, ra, rb, rc):
    half = ROPE_DIM // 2
    return t * ra + pltpu.roll(t, half, 1) * rb + pltpu.roll(t, LANES - half, 1) * rc


def _mla_kernel(q_ref, kn_ref, kp_ref, v_ref, ra_ref, rb_ref, rc_ref, o_ref, kfull_ref,
                *, tile, group):
    dq = NOPE_DIM + LANES

    @pl.when(pl.program_id(2) == 0)
    def _():
        for hd in range(group):
            kfull_ref[:, hd * dq:hd * dq + NOPE_DIM] = kn_ref[:, _head_cols(hd)]
            kfull_ref[:, hd * dq + NOPE_DIM:(hd + 1) * dq] = kp_ref[...]

    qs = []
    for hd in range(group):
        q = q_ref[:, hd * dq:(hd + 1) * dq]
        q_pe = _rope_lanes(q[:, NOPE_DIM:].astype(F32), ra_ref[...], rb_ref[...], rc_ref[...])
        qs.append(jnp.concatenate([q[:, :NOPE_DIM], q_pe.astype(BF16)], axis=1))
    outs = _flash_sweep(
        qs,
        lambda hd, start: kfull_ref[pl.ds(start, tile), hd * dq:(hd + 1) * dq],
        lambda hd, start: v_ref[pl.ds(start, tile), _head_cols(hd)],
        None, pl.program_id(2), tile, V_DIM)
    for hd in range(group):
        o_ref[:, _head_cols(hd)] = outs[hd].astype(o_ref.dtype)


def _mla_attention(q, kv, k_pe, ra, rb, rc, batch, seq, heads):
    n = q.shape[0]
    tile = min(FA_TILE, seq)
    nq = seq // tile
    group = FA_GROUP
    assert heads % group == 0
    ng = heads // group
    dq = NOPE_DIM + LANES
    kern = functools.partial(_mla_kernel, tile=tile, group=group)
    rope_spec = pl.BlockSpec((tile, LANES), lambda b, g, i: (b * nq + i, 0))
    return pl.pallas_call(
        kern,
        out_shape=jax.ShapeDtypeStruct((n, heads * V_DIM), BF16),
        grid=(batch, ng, nq),
        in_specs=[pl.BlockSpec((tile, group * dq), lambda b, g, i: (b * nq + i, g)),
                  pl.BlockSpec((seq, group * NOPE_DIM), lambda b, g, i: (b, g)),
                  pl.BlockSpec((seq, LANES), lambda b, g, i: (b, 0)),
                  pl.BlockSpec((seq, group * V_DIM), lambda b, g, i: (b, ng + g)),
                  rope_spec, rope_spec, rope_spec],
        out_specs=pl.BlockSpec((tile, group * V_DIM), lambda b, g, i: (b * nq + i, g)),
        scratch_shapes=[pltpu.VMEM((seq, group * dq), BF16)],
        compiler_params=_params(("parallel", "parallel", "arbitrary")),
        name="mla_attention",
    )(q, kv, k_pe, kv, ra, rb, rc)


def _mla_prep_kernel(down_ref, pos_ref, qn_ref, kvn_ref, invf_ref,
                     cq_ref, ckv_ref, kpe_ref, ra_ref, rb_ref, rc_ref):
    down = down_ref[...]

    def rms(c, g):
        return c * lax.rsqrt(jnp.mean(c * c, axis=-1, keepdims=True) + RMS_EPS) * g

    cq_ref[...] = rms(down[:, :Q_LORA], qn_ref[...]).astype(cq_ref.dtype)
    ckv_ref[...] = rms(down[:, Q_LORA:Q_LORA + KV_LORA], kvn_ref[...]).astype(ckv_ref.dtype)

    ang = pos_ref[...].astype(F32) * invf_ref[...]
    cos, sin = jnp.cos(ang), jnp.sin(ang)
    lane = lax.broadcasted_iota(jnp.int32, ang.shape, 1)
    half = ROPE_DIM // 2
    ra = jnp.where(lane < ROPE_DIM, cos, 0.0)
    rb = jnp.where((lane >= half) & (lane < ROPE_DIM), sin, 0.0)
    rc = jnp.where(lane < half, -sin, 0.0)
    ra_ref[...] = ra
    rb_ref[...] = rb
    rc_ref[...] = rc
    kpe_ref[...] = _rope_lanes(down[:, Q_LORA + KV_LORA:], ra, rb, rc).astype(kpe_ref.dtype)


def _mla_prep(down, pos, q_norm, kv_norm, inv_freq):
    n, dd = down.shape
    tm = _pick_tile(n, (ROW_TILE, 128))
    row = lambda w: pl.BlockSpec((tm, w), lambda i: (i, 0))
    const = lambda w: pl.BlockSpec((1, w), lambda i: (0, 0))
    return pl.pallas_call(
        _mla_prep_kernel,
        out_shape=(jax.ShapeDtypeStruct((n, Q_LORA), BF16),
                   jax.ShapeDtypeStruct((n, KV_LORA), BF16),
                   jax.ShapeDtypeStruct((n, LANES), BF16),
                   jax.ShapeDtypeStruct((n, LANES), F32),
                   jax.ShapeDtypeStruct((n, LANES), F32),
                   jax.ShapeDtypeStruct((n, LANES), F32)),
        grid=(n // tm,),
        in_specs=[row(dd), row(1), const(Q_LORA), const(KV_LORA), const(LANES)],
        out_specs=(row(Q_LORA), row(KV_LORA), row(LANES), row(LANES), row(LANES), row(LANES)),
        compiler_params=_params(("parallel",)),
        name="mla_prep",
    )(down, pos, q_norm, kv_norm, inv_freq)


def _ln_router_kernel(x_ref, mix_ref, g_ref, b_ref, rw_hi_ref, rw_lo_ref, rb_ref,
                      x1_ref, xp_ref, e_ref, gate_ref, rank_ref, cnt_ref, carry_ref, *, alpha):
    @pl.when(pl.program_id(0) == 0)
    def _():
        carry_ref[...] = jnp.zeros_like(carry_ref)

    x1 = _layer_norm(alpha * x_ref[...] + mix_ref[...], g_ref[...], b_ref[...])
    x1_ref[...] = x1
    _store_packed_rows(xp_ref, x1)

    a_hi, a_lo = _split2(x1)
    rw_hi, rw_lo = rw_hi_ref[...], rw_lo_ref[...]
    logits = (jnp.dot(a_hi, rw_hi, preferred_element_type=F32)
              + jnp.dot(a_hi, rw_lo, preferred_element_type=F32)
              + jnp.dot(a_lo, rw_hi, preferred_element_type=F32)) + rb_ref[...]

    tm = logits.shape[0]
    lane = lax.broadcasted_iota(jnp.int32, (tm, LANES), 1)
    work = logits
    picks, vals = [], []
    for _ in range(TOP_K):
        mx = jnp.max(work, axis=-1, keepdims=True)
        idx = jnp.min(jnp.where(work == mx, lane, LANES), axis=-1, keepdims=True)
        pick = lane == idx
        picks.append(pick)
        vals.append(mx)
        work = jnp.where(pick, -jnp.inf, work)

    exps = [jnp.exp(v - vals[0]) for v in vals]
    denom = exps[0]
    for e in exps[1:]:
        denom = denom + e

    chosen = jnp.zeros((tm, LANES), F32)
    for pick in picks:
        chosen = chosen + jnp.where(pick, 1.0, 0.0)
    row = lax.broadcasted_iota(jnp.int32, (tm, tm), 0)
    col = lax.broadcasted_iota(jnp.int32, (tm, tm), 1)
    strict_lower = jnp.where(col < row, 1.0, 0.0).astype(BF16)
    before = jnp.dot(strict_lower, chosen.astype(BF16), preferred_element_type=F32) + carry_ref[...]

    e_out = jnp.zeros((tm, LANES), jnp.int32)
    gate_out = jnp.zeros((tm, LANES), F32)
    rank_out = jnp.zeros((tm, LANES), F32)
    for k in range(TOP_K):
        idx = jnp.min(jnp.where(picks[k], lane, LANES), axis=-1, keepdims=True)
        rank = jnp.sum(jnp.where(picks[k], before, 0.0), axis=-1, keepdims=True)
        e_out = jnp.where(lane == k, idx, e_out)
        gate_out = jnp.where(lane == k, exps[k] / denom, gate_out)
        rank_out = jnp.where(lane == k, rank, rank_out)
    e_ref[...] = e_out
    gate_ref[...] = gate_out
    rank_ref[...] = rank_out.astype(jnp.int32)

    carry_ref[...] = carry_ref[...] + jnp.sum(chosen, axis=0, keepdims=True)
    cnt_ref[...] = carry_ref[...]


def _ln_router(x, mix, g, b, rw_hi, rw_lo, rb, alpha):
    n, d = x.shape
    tm = _pick_tile(n, (ROW_TILE, 128))
    row = lambda w: pl.BlockSpec((tm, w), lambda i: (i, 0))
    const = lambda r, w: pl.BlockSpec((r, w), lambda i: (0, 0))
    return pl.pallas_call(
        functools.partial(_ln_router_kernel, alpha=alpha),
        out_shape=(jax.ShapeDtypeStruct((n, d), F32),
                   jax.ShapeDtypeStruct((n, d // 2), jnp.uint32),
                   jax.ShapeDtypeStruct((n, LANES), jnp.int32),
                   jax.ShapeDtypeStruct((n, LANES), F32),
                   jax.ShapeDtypeStruct((n, LANES), jnp.int32),
                   jax.ShapeDtypeStruct((1, LANES), F32)),
        grid=(n // tm,),
        in_specs=[row(d), row(d), const(1, d), const(1, d), const(d, LANES), const(d, LANES),
                  const(1, LANES)],
        out_specs=(row(d), row(d // 2), row(LANES), row(LANES), row(LANES), const(1, LANES)),
        scratch_shapes=[pltpu.VMEM((1, LANES), F32)],
        compiler_params=_params(("arbitrary",)),
        name="ln_router",
    )(x, mix, g, b, rw_hi, rw_lo, rb)


SUBLANES = 8


def _spread_rows(j, rows):
    tiles = rows // SUBLANES
    return (j % tiles) * SUBLANES + j // tiles


def _expert_kernel(blk_e_ref, n_used_ref, row_tok_ref,
                   x_hbm, wgu_ref, bgu_ref, wdn_ref, bdn_ref, y_ref, xbuf, sem, *, bm, d_expert):
    i = pl.program_id(0)
    n_used = n_used_ref[0]

    def gather_copy(tok, r, slot):
        return pltpu.make_async_copy(x_hbm.at[pl.ds(tok, 1)], xbuf.at[slot, pl.ds(r, 1)], sem.at[slot])

    def issue(blk, slot):
        def body(j, _):
            r = _spread_rows(j, bm)
            gather_copy(row_tok_ref[blk * bm + r], r, slot).start()
            return 0
        lax.fori_loop(0, bm, body, 0, unroll=8)

    @pl.when(jnp.logical_and(i == 0, n_used > 0))
    def _():
        issue(0, 0)

    @pl.when(i + 1 < n_used)
    def _():
        issue(i + 1, (i + 1) % 2)

    @pl.when(i < n_used)
    def _():
        slot = i % 2

        def wait_body(r, _):
            gather_copy(0, r, slot).wait()
            return 0
        lax.fori_loop(0, bm, wait_body, 0, unroll=8)

        xb = _unpack_rows(xbuf[slot])
        hb = jnp.dot(xb, wgu_ref[...], preferred_element_type=F32) + bgu_ref[...]
        g = jnp.minimum(hb[:, :d_expert], SWIGLU_LIMIT)
        u = jnp.clip(hb[:, d_expert:], -SWIGLU_LIMIT, SWIGLU_LIMIT)
        act = (u + 1.0) * (g * (1.0 / (1.0 + jnp.exp(-SWIGLU_ALPHA * g))))
        y_ref[...] = jnp.dot(act.astype(BF16), wdn_ref[...], preferred_element_type=F32) + bdn_ref[...]

    @pl.when(i >= n_used)
    def _():
        y_ref[...] = jnp.zeros_like(y_ref)


def _experts(xp, blk_e, n_used, row_tok, w_gu, b_gu, w_dn, b_dn, layer, bm):
    _, _, d, f2 = w_gu.shape
    d_expert = f2 // 2
    n_rows = row_tok.shape[0]
    n_blk = n_rows // bm
    wmap = lambda i, be, nu, rt: (layer, be[i], 0, 0)
    grid_spec = pltpu.PrefetchScalarGridSpec(
        num_scalar_prefetch=3,
        grid=(n_blk,),
        in_specs=[pl.BlockSpec(memory_space=pl.ANY),
                  pl.BlockSpec((None, None, d, f2), wmap),
                  pl.BlockSpec((None, None, 1, f2), wmap),
                  pl.BlockSpec((None, None, d_expert, d), wmap),
                  pl.BlockSpec((None, None, 1, d), wmap)],
        out_specs=pl.BlockSpec((bm, d), lambda i, be, nu, rt: (i, 0)),
        scratch_shapes=[pltpu.VMEM((2, bm, d // 2), jnp.uint32), pltpu.SemaphoreType.DMA((2,))],
    )
    return pl.pallas_call(
        functools.partial(_expert_kernel, bm=bm, d_expert=d_expert),
        out_shape=jax.ShapeDtypeStruct((n_rows, d), F32),
        grid_spec=grid_spec,
        compiler_params=_params(("arbitrary",)),
        name="experts",
    )(blk_e, n_used, row_tok, xp, w_gu, b_gu[:, :, None, :], w_dn, b_dn[:, :, None, :])


def _combine_kernel(pos_ref, y_hbm, x1_ref, gate_ref, g_ref, b_ref, x2_ref, xb_ref, ybuf, sem, *, tm, alpha):
    i = pl.program_id(0)
    n_steps = pl.num_programs(0)

    def gather_copy(src_row, k, t, slot):
        return pltpu.make_async_copy(y_hbm.at[pl.ds(src_row, 1)], ybuf.at[slot, k, pl.ds(t, 1)], sem.at[slot])

    def issue(step, slot):
        def body(j, _):
            t = _spread_rows(j, tm)
            for k in range(TOP_K):
                gather_copy(pos_ref[(step * tm + t) * TOP_K + k], k, t, slot).start()
            return 0
        lax.fori_loop(0, tm, body, 0, unroll=2)

    @pl.when(i == 0)
    def _():
        issue(0, 0)

    @pl.when(i + 1 < n_steps)
    def _():
        issue(i + 1, (i + 1) % 2)

    slot = i % 2

    def wait_body(t, _):
        for k in range(TOP_K):
            gather_copy(0, k, t, slot).wait()
        return 0
    lax.fori_loop(0, tm, wait_body, 0, unroll=2)

    gate = gate_ref[...]
    ffn = gate[:, 0:1] * ybuf[slot, 0]
    for k in range(1, TOP_K):
        ffn = ffn + gate[:, k:k + 1] * ybuf[slot, k]
    x2 = _layer_norm(alpha * x1_ref[...] + ffn, g_ref[...], b_ref[...])
    x2_ref[...] = x2
    xb_ref[...] = x2.astype(BF16)


def _combine_ln(pos_flat, y, x1, gate, g, b, alpha):
    n, d = x1.shape
    tm = _pick_tile(n, (COMBINE_TM,))
    grid_spec = pltpu.PrefetchScalarGridSpec(
        num_scalar_prefetch=1,
        grid=(n // tm,),
        in_specs=[pl.BlockSpec(memory_space=pl.ANY),
                  pl.BlockSpec((tm, d), lambda i, p: (i, 0)),
                  pl.BlockSpec((tm, LANES), lambda i, p: (i, 0)),
                  pl.BlockSpec((1, d), lambda i, p: (0, 0)),
                  pl.BlockSpec((1, d), lambda i, p: (0, 0))],
        out_specs=(pl.BlockSpec((tm, d), lambda i, p: (i, 0)),
                   pl.BlockSpec((tm, d), lambda i, p: (i, 0))),
        scratch_shapes=[pltpu.VMEM((2, TOP_K, tm, d), F32), pltpu.SemaphoreType.DMA((2,))],
    )
    return pl.pallas_call(
        functools.partial(_combine_kernel, tm=tm, alpha=alpha),
        out_shape=(jax.ShapeDtypeStruct((n, d), F32), jax.ShapeDtypeStruct((n, d), BF16)),
        grid_spec=grid_spec,
        compiler_params=_params(("arbitrary",)),
        name="combine_ln",
    )(pos_flat, y, x1, gate, g, b)


def _pad_cols(w, width):
    return jnp.pad(w, ((0, 0), (0, width - w.shape[1])))


def _moe_layer(x, mix, ln_g, ln_b, router_w, router_b, w_gu, b_gu, w_dn, b_dn, layer, ln2_g, ln2_b, alpha):
    n, d = x.shape
    n_exp = router_w.shape[1]
    bm = EXPERT_BM

    rw = _pad_cols(router_w, LANES)
    rw_hi = rw.astype(BF16)
    rw_lo = (rw - rw_hi.astype(F32)).astype(BF16)
    rb = jnp.full((1, LANES), MASK_VALUE, F32).at[0, :n_exp].set(router_b)
    x1, xp, top_e, gate, rank, cnt = _ln_router(x, mix, ln_g[None], ln_b[None], rw_hi, rw_lo, rb, alpha)

    counts = cnt[0, :n_exp].astype(jnp.int32)
    padded = (counts + bm - 1) // bm * bm
    pend = jnp.cumsum(padded)
    pstart = pend - padded
    top_e = top_e[:, :TOP_K]
    pos = (pstart[top_e] + rank[:, :TOP_K]).reshape(-1)
    n_blk = -(-(n * TOP_K + n_exp * (bm - 1)) // bm)
    n_rows = n_blk * bm
    tok = jnp.repeat(jnp.arange(n, dtype=jnp.int32), TOP_K)
    row_tok = jnp.zeros((n_rows,), jnp.int32).at[pos].set(tok)
    blk_start = jnp.arange(n_blk, dtype=jnp.int32) * bm
    blk_e = jnp.minimum(jnp.sum((pend[None, :] <= blk_start[:, None]).astype(jnp.int32), axis=1), n_exp - 1)
    n_used = (pend[-1:] // bm).astype(jnp.int32)

    y = _experts(xp, blk_e, n_used, row_tok, w_gu, b_gu, w_dn, b_dn, layer, bm)
    return _combine_ln(pos, y, x1, gate, ln2_g[None], ln2_b[None], alpha)


def _even_mixer(xb, w_in, b_f, w_o, batch, seq):
    sbw = SB_HEADS * HEAD_DIM
    fxw = FOX_HEADS * HEAD_DIM
    main = 3 * sbw + 3 * fxw
    zscale = HEAD_DIM ** -0.5 * LOG2E
    col = jnp.arange(main)
    is_q = (col < sbw) | ((col >= 3 * sbw) & (col < 3 * sbw + fxw))
    h = _matmul(xb, (w_in[:, :main] * jnp.where(is_q, zscale, 1.0)).astype(BF16), BF16)
    w_f = _pad_cols(w_in[:, main:], LANES).astype(BF16)
    cum = _fox_gate(xb, w_f, _pad_cols(b_f[None], LANES), batch, seq)
    o_sb = _sb_attention(h, batch, seq, SB_HEADS, 0, SB_HEADS, 2 * SB_HEADS)
    c0 = 3 * SB_HEADS
    o_fx = _fox_attention(h, cum, batch, seq, FOX_HEADS, c0, c0 + FOX_HEADS, c0 + 2 * FOX_HEADS)
    o = jnp.concatenate([o_sb, o_fx], axis=-1)
    return _matmul(o, w_o.astype(BF16), F32)


def _mla_mixer(xb, pos, w_down, q_norm, kv_norm, w_uq, w_ukv, w_o, batch, seq):
    hds = MLA_HEADS
    w_down_p = _pad_cols(w_down, Q_LORA + KV_LORA + LANES).astype(BF16)
    down = _matmul(xb, w_down_p, F32)
    freq = ROPE_THETA ** (-jnp.arange(0, ROPE_DIM, 2, dtype=F32) / ROPE_DIM)
    inv_freq = jnp.concatenate([freq, freq, jnp.zeros((LANES - ROPE_DIM,), F32)])[None]
    cq, ckv, k_pe, ra, rb, rc = _mla_prep(down, pos, q_norm[None], kv_norm[None], inv_freq)

    wq = w_uq.reshape(Q_LORA, hds, NOPE_DIM + ROPE_DIM)
    wq = jnp.pad(wq, ((0, 0), (0, 0), (0, LANES - ROPE_DIM))).reshape(Q_LORA, hds * (NOPE_DIM + LANES))
    wkv = w_ukv.reshape(KV_LORA, hds, NOPE_DIM + V_DIM)
    wkv = jnp.concatenate([wkv[:, :, :NOPE_DIM].reshape(KV_LORA, hds * NOPE_DIM),
                           wkv[:, :, NOPE_DIM:].reshape(KV_LORA, hds * V_DIM)], axis=1)
    q = _matmul(cq, (wq * ((NOPE_DIM + ROPE_DIM) ** -0.5 * LOG2E)).astype(BF16), BF16)
    kv = _matmul(ckv, wkv.astype(BF16), BF16)
    o = _mla_attention(q, kv, k_pe, ra, rb, rc, batch, seq, hds)
    return _matmul(o, w_o.astype(BF16), F32)


def kernel(x, positions, ln_mix_g, ln_mix_b, ln_ffn_g, ln_ffn_b, even_w_in, fox_b_f, even_w_o, mla_w_down, mla_q_norm, mla_kv_norm, mla_w_uq, mla_w_ukv, mla_w_o, router_w, router_b, expert_w_gate_up, expert_b_gate_up, expert_w_down, expert_b_down):
    batch, seq, d = x.shape
    depth = ln_mix_g.shape[0]
    alpha = (2 * depth) ** 0.25
    n = batch * seq
    xf = x.reshape(n, d)
    xb = xf.astype(BF16)
    pos = positions.reshape(n, 1)
    w_gu = expert_w_gate_up.astype(BF16)
    w_dn = expert_w_down.astype(BF16)
    for layer in range(depth):
        i = layer // 2
        if layer % 2 == 0:
            mix = _even_mixer(xb, even_w_in[i], fox_b_f[i], even_w_o[i], batch, seq)
        else:
            mix = _mla_mixer(xb, pos, mla_w_down[i], mla_q_norm[i], mla_kv_norm[i],
                             mla_w_uq[i], mla_w_ukv[i], mla_w_o[i], batch, seq)
        xf, xb = _moe_layer(xf, mix, ln_mix_g[layer], ln_mix_b[layer], router_w[layer], router_b[layer],
                            w_gu, expert_b_gate_up, w_dn, expert_b_down, layer,
                            ln_ffn_g[layer], ln_ffn_b[layer], alpha)
    return xf.reshape(batch, seq, d)
```

```python
import functools

import jax
import jax.numpy as jnp
from jax import lax
from jax.experimental import pallas as pl
from jax.experimental.pallas import tpu as pltpu

F32 = jnp.float32
BF16 = jnp.bfloat16

HEAD_DIM = 128
SB_HEADS = 8
FOX_HEADS = 8
MLA_HEADS = 16
Q_LORA = 512
KV_LORA = 512
NOPE_DIM = 128
ROPE_DIM = 64
V_DIM = 128
ROPE_THETA = 10000.0
N_EXPERTS = 32
TOP_K = 4
SWIGLU_LIMIT = 7.0
SWIGLU_ALPHA = 1.702
LN_EPS = 1e-5
RMS_EPS = 1e-6

LANES = 128
MASK_VALUE = -1e30
VMEM_LIMIT = 56 * 1024 * 1024
LOG2E = 1.4426950408889634
LN2 = 0.6931471805599453

MM_TM = 1024
MM_TN = 512
SB_TQ = 512
SB_CK = 256
SB_GROUP = 4
FA_TILE = 512
FA_GROUP = 4
ROW_TILE = 256
GATE_TILE = 512
EXPERT_BM = 512
COMBINE_TM = 128


def _params(sem):
    return pltpu.CompilerParams(dimension_semantics=sem, vmem_limit_bytes=VMEM_LIMIT)


def _pick_tile(n, prefs):
    for t in prefs:
        if n % t == 0:
            return t
    return n


def _mm_kernel(a_ref, w_ref, o_ref):
    o_ref[...] = jnp.dot(a_ref[...].astype(BF16), w_ref[...],
                         preferred_element_type=F32).astype(o_ref.dtype)


def _matmul(a, w, out_dtype):
    m, k = a.shape
    n = w.shape[1]
    tm = _pick_tile(m, (MM_TM, 512, 256, 128))
    tn = _pick_tile(n, (MM_TN, 384, 256, 128))
    return pl.pallas_call(
        _mm_kernel,
        out_shape=jax.ShapeDtypeStruct((m, n), out_dtype),
        grid=(m // tm, n // tn),
        in_specs=[pl.BlockSpec((tm, k), lambda i, j: (i, 0)),
                  pl.BlockSpec((k, tn), lambda i, j: (0, j))],
        out_specs=pl.BlockSpec((tm, tn), lambda i, j: (i, j)),
        compiler_params=_params(("parallel", "parallel")),
        name="matmul",
    )(a, w)


def _split2(x):
    hi = x.astype(BF16)
    lo = (x - hi.astype(F32)).astype(BF16)
    return hi, lo


def _split3(x):
    h1 = x.astype(BF16)
    r1 = x - h1.astype(F32)
    h2 = r1.astype(BF16)
    h3 = (r1 - h2.astype(F32)).astype(BF16)
    return h1, h2, h3


def _log1pexp_neg_abs(z):
    return jnp.log(1.0 + jnp.exp(-jnp.abs(z)))


def _neg_abs(x):
    bits = lax.bitcast_convert_type(x, jnp.uint32) | jnp.uint32(0x80000000)
    return lax.bitcast_convert_type(bits, F32)


def _store_packed_rows(xp_ref, x):
    half = x.shape[1] // 2
    lo = lax.bitcast_convert_type(x[:, :half].astype(BF16).astype(F32), jnp.uint32)
    hi = lax.bitcast_convert_type(x[:, half:].astype(BF16).astype(F32), jnp.uint32)
    w = (lo >> 16) | (hi & jnp.uint32(0xFFFF0000))
    for c in range(half // LANES):
        xp_ref[:, c, :] = w[:, c * LANES:(c + 1) * LANES]


def _unpack_rows(planes):
    los = [lax.bitcast_convert_type(w << 16, F32).astype(BF16) for w in planes]
    his = [lax.bitcast_convert_type(w & jnp.uint32(0xFFFF0000), F32).astype(BF16) for w in planes]
    return jnp.concatenate(los + his, axis=1)


def _layer_norm(h, g, b):
    mu = jnp.mean(h, axis=-1, keepdims=True)
    c = h - mu
    var = jnp.mean(c * c, axis=-1, keepdims=True)
    return c * lax.rsqrt(var + LN_EPS) * g + b


def _head_cols(hd):
    return slice(hd * HEAD_DIM, (hd + 1) * HEAD_DIM)


def _fox_gate_kernel(x_ref, w_ref, b_ref, o_ref, carry_ref):
    @pl.when(pl.program_id(1) == 0)
    def _():
        carry_ref[...] = jnp.zeros_like(carry_ref)

    f = jnp.dot(x_ref[...], w_ref[...], preferred_element_type=F32) + b_ref[...]
    log_f = jnp.minimum(f, 0.0) - _log1pexp_neg_abs(f)
    ts = log_f.shape[0]
    row = lax.broadcasted_iota(jnp.int32, (ts, ts), 0)
    col = lax.broadcasted_iota(jnp.int32, (ts, ts), 1)
    tri = jnp.where(col <= row, 1.0, 0.0).astype(BF16)
    cum = carry_ref[...]
    for part in _split3(log_f):
        cum = cum + jnp.dot(tri, part, preferred_element_type=F32)
    o_ref[...] = cum
    carry_ref[...] = cum[ts - 1:ts, :]


def _fox_gate(xb, w_f, b_f, batch, seq):
    n, d = xb.shape
    ts = _pick_tile(seq, (GATE_TILE, 256, 128))
    ns = seq // ts
    return pl.pallas_call(
        _fox_gate_kernel,
        out_shape=jax.ShapeDtypeStruct((n, LANES), F32),
        grid=(batch, ns),
        in_specs=[pl.BlockSpec((ts, d), lambda b, j: (b * ns + j, 0)),
                  pl.BlockSpec((d, LANES), lambda b, j: (0, 0)),
                  pl.BlockSpec((1, LANES), lambda b, j: (0, 0))],
        out_specs=pl.BlockSpec((ts, LANES), lambda b, j: (b * ns + j, 0)),
        scratch_shapes=[pltpu.VMEM((1, LANES), F32)],
        compiler_params=_params(("parallel", "arbitrary")),
        name="fox_gate",
    )(xb, w_f, b_f)


def _sb_kernel(q_ref, k_ref, v_ref, u_ref, o_ref, *, tq, ck, group):
    qi = pl.program_id(2)
    n_diag = tq // ck
    n_below = qi * n_diag

    def evaluate(units):
        n = len(units)
        starts = [pl.multiple_of(kb * ck, ck) for _, kb, _, _ in units]
        zs, sps, sums, stricts, laters = [None] * n, [None] * n, [None] * n, [None] * n, [None] * n
        pvs, tots = [None] * n, [None] * n

        def scores(u):
            hd, _, r0, masked = units[u]
            z = lax.dot_general(q_ref[r0:, _head_cols(hd)], k_ref[pl.ds(starts[u], ck), _head_cols(hd)],
                                (((1,), (1,)), ((), ())), preferred_element_type=F32)
            sp = jnp.maximum(z, 0.0) + jnp.log(1.0 + jnp.exp2(_neg_abs(z))) * (1.0 / LN2)
            zs[u], sps[u], sums[u] = z, sp, sp
            if masked:
                q_pos = qi * tq + r0 + lax.broadcasted_iota(jnp.int32, z.shape, 0)
                stricts[u] = (starts[u] + lax.broadcasted_iota(jnp.int32, z.shape, 1)) < q_pos
                sums[u] = jnp.where(stricts[u], sp, 0.0)

        def prefix(u):
            laters[u] = jnp.dot(sums[u].astype(BF16), u_ref[...], preferred_element_type=F32)

        def values(u):
            hd = units[u][0]
            w = jnp.exp2(zs[u] - sps[u] - laters[u])
            if stricts[u] is not None:
                w = jnp.where(stricts[u], w, 0.0)
            pvs[u] = jnp.dot(w.astype(BF16), v_ref[pl.ds(starts[u], ck), _head_cols(hd)],
                             preferred_element_type=F32)
            tots[u] = laters[u][:, :1] + sums[u][:, :1]

        for t in range(n + 2):
            if t >= 2:
                values(t - 2)
            if 1 <= t <= n:
                prefix(t - 1)
            if t < n:
                scores(t)
        return pvs, tots

    def fold(acc, later, pv, tot, r0):
        acc_s = acc[r0:] + pv * jnp.exp2(-later[r0:])
        later_s = later[r0:] + tot
        if r0:
            return (jnp.concatenate([acc[:r0], acc_s], axis=0),
                    jnp.concatenate([later[:r0], later_s], axis=0))
        return acc_s, later_s

    diag_units = [(hd, n_below + d, d * ck, True) for hd in range(group) for d in range(n_diag - 1, -1, -1)]
    pvs, tots = evaluate(diag_units)
    accs, laters = [], []
    for hd in range(group):
        acc, later = jnp.zeros((tq, HEAD_DIM), F32), jnp.zeros((tq, 1), F32)
        for u in range(hd * n_diag, (hd + 1) * n_diag):
            acc, later = fold(acc, later, pvs[u], tots[u], diag_units[u][2])
        accs.append(acc)
        laters.append(later)

    def body(i, carry):
        accs, laters = carry
        kb = n_below - 1 - 2 * i
        pvs, tots = evaluate([(hd, kb - c, 0, False) for hd in range(group) for c in range(2)])
        out = [fold(*fold(accs[hd], laters[hd], pvs[2 * hd], tots[2 * hd], 0),
                    pvs[2 * hd + 1], tots[2 * hd + 1], 0) for hd in range(group)]
        return tuple(o[0] for o in out), tuple(o[1] for o in out)

    accs, _ = lax.fori_loop(0, n_below // 2, body, (tuple(accs), tuple(laters)))
    for hd in range(group):
        o_ref[:, _head_cols(hd)] = accs[hd].astype(o_ref.dtype)


def _sb_attention(h, batch, seq, heads, q_col, k_col, v_col):
    n = h.shape[0]
    tq, ck = min(SB_TQ, seq), min(SB_CK, seq // 2)
    nq = seq // tq
    group = SB_GROUP
    assert (tq // ck) % 2 == 0 and heads % group == 0
    assert q_col % group == 0 and k_col % group == 0 and v_col % group == 0
    gw = group * HEAD_DIM
    j = lax.broadcasted_iota(jnp.int32, (ck, ck), 0)
    s = lax.broadcasted_iota(jnp.int32, (ck, ck), 1)
    u = jnp.where(j > s, 1.0, 0.0).astype(BF16)
    kern = functools.partial(_sb_kernel, tq=tq, ck=ck, group=group)
    return pl.pallas_call(
        kern,
        out_shape=jax.ShapeDtypeStruct((n, heads * HEAD_DIM), BF16),
        grid=(batch, heads // group, nq),
        in_specs=[pl.BlockSpec((tq, gw), lambda b, g, i: (b * nq + i, q_col // group + g)),
                  pl.BlockSpec((seq, gw), lambda b, g, i: (b, k_col // group + g)),
                  pl.BlockSpec((seq, gw), lambda b, g, i: (b, v_col // group + g)),
                  pl.BlockSpec((ck, ck), lambda b, g, i: (0, 0))],
        out_specs=pl.BlockSpec((tq, gw), lambda b, g, i: (b * nq + i, g)),
        compiler_params=_params(("parallel", "parallel", "arbitrary")),
        name="sb_attention",
    )(h, h, h, u)


def _flash_sweep(qs, k_fn, v_fn, bias_fn, qi, tile, dv):
    n = len(qs)

    def sweep_step(kb, states, masked):
        start = pl.multiple_of(kb * tile, tile)
        scores, new = [None] * n, [None] * n

        def score(hd):
            s = lax.dot_general(qs[hd], k_fn(hd, start), (((1,), (1,)), ((), ())), preferred_element_type=F32)
            if bias_fn is not None:
                s = s + bias_fn(hd, kb)
            if masked:
                s = jnp.where(lax.broadcasted_iota(jnp.int32, s.shape, 1)
                              <= lax.broadcasted_iota(jnp.int32, s.shape, 0), s, MASK_VALUE)
            scores[hd] = s

        def update(hd):
            m, l, acc = states[hd]
            s = scores[hd]
            m_new = jnp.maximum(m, jnp.max(s, axis=-1, keepdims=True))
            p = jnp.exp2(s - m_new)
            corr = jnp.exp2(m - m_new)
            l = corr * l + jnp.sum(p, axis=-1, keepdims=True)
            acc = corr * acc + jnp.dot(p.astype(BF16), v_fn(hd, start), preferred_element_type=F32)
            new[hd] = (m_new, l, acc)

        for i in range(n + 1):
            if i < n:
                score(i)
            if i >= 1:
                update(i - 1)
        return tuple(new)

    init = (jnp.full((tile, 1), MASK_VALUE, F32), jnp.zeros((tile, 1), F32), jnp.zeros((tile, dv), F32))
    states = lax.fori_loop(0, qi, lambda kb, st: sweep_step(kb, st, False), tuple(init for _ in range(n)))
    states = sweep_step(qi, states, True)
    return [acc / l for _, l, acc in states]


def _fox_kernel(q_ref, k_ref, v_ref, ck_ref, o_ref, *, tile, group):
    qs = [q_ref[:, _head_cols(hd)] for hd in range(group)]
    outs = _flash_sweep(
        qs,
        lambda hd, start: k_ref[pl.ds(start, tile), _head_cols(hd)],
        lambda hd, start: v_ref[pl.ds(start, tile), _head_cols(hd)],
        lambda hd, kb: ck_ref[hd, pl.ds(kb, 1), :] * (-LOG2E),
        pl.program_id(2), tile, HEAD_DIM)
    for hd in range(group):
        o_ref[:, _head_cols(hd)] = outs[hd].astype(o_ref.dtype)


def _fox_attention(h, cum, batch, seq, heads, q_col, k_col, v_col):
    n = h.shape[0]
    tile = min(FA_TILE, seq)
    nq = seq // tile
    group = FA_GROUP
    assert heads % group == 0 and q_col % group == 0 and k_col % group == 0 and v_col % group == 0
    gw = group * HEAD_DIM
    cum_k = jnp.transpose(cum[:, :heads].reshape(batch, seq, heads), (0, 2, 1)).reshape(batch, heads, nq, tile)
    kern = functools.partial(_fox_kernel, tile=tile, group=group)
    return pl.pallas_call(
        kern,
        out_shape=jax.ShapeDtypeStruct((n, heads * HEAD_DIM), BF16),
        grid=(batch, heads // group, nq),
        in_specs=[pl.BlockSpec((tile, gw), lambda b, g, i: (b * nq + i, q_col // group + g)),
                  pl.BlockSpec((seq, gw), lambda b, g, i: (b, k_col // group + g)),
                  pl.BlockSpec((seq, gw), lambda b, g, i: (b, v_col // group + g)),
                  pl.BlockSpec((None, group, nq, tile), lambda b, g, i: (b, g, 0, 0))],
        out_specs=pl.BlockSpec((tile, gw), lambda b, g, i: (b * nq + i, g)),
        compiler_params=_params(("parallel", "parallel", "arbitrary")),
        name="fox_attention",
    )(h, h, h, cum_k)


def _rope_lanes(t, ra, rb, rc):
    half = ROPE_DIM // 2
    return t * ra + pltpu.roll(t, half, 1) * rb + pltpu.roll(t, LANES - half, 1) * rc


def _mla_kernel(q_ref, kn_ref, kp_ref, v_ref, ra_ref, rb_ref, rc_ref, o_ref, kfull_ref,
                *, tile, group):
    dq = NOPE_DIM + LANES

    @pl.when(pl.program_id(2) == 0)
    def _():
        for hd in range(group):
            kfull_ref[:, hd * dq:hd * dq + NOPE_DIM] = kn_ref[:, _head_cols(hd)]
            kfull_ref[:, hd * dq + NOPE_DIM:(hd + 1) * dq] = kp_ref[...]

    qs = []
    for hd in range(group):
        q = q_ref[:, hd * dq:(hd + 1) * dq]
        q_pe = _rope_lanes(q[:, NOPE_DIM:].astype(F32), ra_ref[...], rb_ref[...], rc_ref[...])
        qs.append(jnp.concatenate([q[:, :NOPE_DIM], q_pe.astype(BF16)], axis=1))
    outs = _flash_sweep(
        qs,
        lambda hd, start: kfull_ref[pl.ds(start, tile), hd * dq:(hd + 1) * dq],
        lambda hd, start: v_ref[pl.ds(start, tile), _head_cols(hd)],
        None, pl.program_id(2), tile, V_DIM)
    for hd in range(group):
        o_ref[:, _head_cols(hd)] = outs[hd].astype(o_ref.dtype)


def _mla_attention(q, kv, k_pe, ra, rb, rc, batch, seq, heads):
    n = q.shape[0]
    tile = min(FA_TILE, seq)
    nq = seq // tile
    group = FA_GROUP
    assert heads % group == 0
    ng = heads // group
    dq = NOPE_DIM + LANES
    kern = functools.partial(_mla_kernel, tile=tile, group=group)
    rope_spec = pl.BlockSpec((tile, LANES), lambda b, g, i: (b * nq + i, 0))
    return pl.pallas_call(
        kern,
        out_shape=jax.ShapeDtypeStruct((n, heads * V_DIM), BF16),
        grid=(batch, ng, nq),
        in_specs=[pl.BlockSpec((tile, group * dq), lambda b, g, i: (b * nq + i, g)),
                  pl.BlockSpec((seq, group * NOPE_DIM), lambda b, g, i: (b, g)),
                  pl.BlockSpec((seq, LANES), lambda b, g, i: (b, 0)),
                  pl.BlockSpec((seq, group * V_DIM), lambda b, g, i: (b, ng + g)),
                  rope_spec, rope_spec, rope_spec],
        out_specs=pl.BlockSpec((tile, group * V_DIM), lambda b, g, i: (b * nq + i, g)),
        scratch_shapes=[pltpu.VMEM((seq, group * dq), BF16)],
        compiler_params=_params(("parallel", "parallel", "arbitrary")),
        name="mla_attention",
    )(q, kv, k_pe, kv, ra, rb, rc)


def _mla_prep_kernel(down_ref, pos_ref, qn_ref, kvn_ref, invf_ref,
                     cq_ref, ckv_ref, kpe_ref, ra_ref, rb_ref, rc_ref):
    down = down_ref[...]

    def rms(c, g):
        return c * lax.rsqrt(jnp.mean(c * c, axis=-1, keepdims=True) + RMS_EPS) * g

    cq_ref[...] = rms(down[:, :Q_LORA], qn_ref[...]).astype(cq_ref.dtype)
    ckv_ref[...] = rms(down[:, Q_LORA:Q_LORA + KV_LORA], kvn_ref[...]).astype(ckv_ref.dtype)

    ang = pos_ref[...].astype(F32) * invf_ref[...]
    cos, sin = jnp.cos(ang), jnp.sin(ang)
    lane = lax.broadcasted_iota(jnp.int32, ang.shape, 1)
    half = ROPE_DIM // 2
    ra = jnp.where(lane < ROPE_DIM, cos, 0.0)
    rb = jnp.where((lane >= half) & (lane < ROPE_DIM), sin, 0.0)
    rc = jnp.where(lane < half, -sin, 0.0)
    ra_ref[...] = ra
    rb_ref[...] = rb
    rc_ref[...] = rc
    kpe_ref[...] = _rope_lanes(down[:, Q_LORA + KV_LORA:], ra, rb, rc).astype(kpe_ref.dtype)


def _mla_prep(down, pos, q_norm, kv_norm, inv_freq):
    n, dd = down.shape
    tm = _pick_tile(n, (ROW_TILE, 128))
    row = lambda w: pl.BlockSpec((tm, w), lambda i: (i, 0))
    const = lambda w: pl.BlockSpec((1, w), lambda i: (0, 0))
    return pl.pallas_call(
        _mla_prep_kernel,
        out_shape=(jax.ShapeDtypeStruct((n, Q_LORA), BF16),
                   jax.ShapeDtypeStruct((n, KV_LORA), BF16),
                   jax.ShapeDtypeStruct((n, LANES), BF16),
                   jax.ShapeDtypeStruct((n, LANES), F32),
                   jax.ShapeDtypeStruct((n, LANES), F32),
                   jax.ShapeDtypeStruct((n, LANES), F32)),
        grid=(n // tm,),
        in_specs=[row(dd), row(1), const(Q_LORA), const(KV_LORA), const(LANES)],
        out_specs=(row(Q_LORA), row(KV_LORA), row(LANES), row(LANES), row(LANES), row(LANES)),
        compiler_params=_params(("parallel",)),
        name="mla_prep",
    )(down, pos, q_norm, kv_norm, inv_freq)


def _ln_router_kernel(x_ref, mix_ref, g_ref, b_ref, rw_hi_ref, rw_lo_ref, rb_ref,
                      x1_ref, xp_ref, e_ref, gate_ref, rank_ref, cnt_ref, carry_ref, *, alpha):
    @pl.when(pl.program_id(0) == 0)
    def _():
        carry_ref[...] = jnp.zeros_like(carry_ref)

    x1 = _layer_norm(alpha * x_ref[...] + mix_ref[...], g_ref[...], b_ref[...])
    x1_ref[...] = x1
    _store_packed_rows(xp_ref, x1)

    a_hi, a_lo = _split2(x1)
    rw_hi, rw_lo = rw_hi_ref[...], rw_lo_ref[...]
    logits = (jnp.dot(a_hi, rw_hi, preferred_element_type=F32)
              + jnp.dot(a_hi, rw_lo, preferred_element_type=F32)
              + jnp.dot(a_lo, rw_hi, preferred_element_type=F32)) + rb_ref[...]

    tm = logits.shape[0]
    lane = lax.broadcasted_iota(jnp.int32, (tm, LANES), 1)
    work = logits
    picks, vals = [], []
    for _ in range(TOP_K):
        mx = jnp.max(work, axis=-1, keepdims=True)
        idx = jnp.min(jnp.where(work == mx, lane, LANES), axis=-1, keepdims=True)
        pick = lane == idx
        picks.append(pick)
        vals.append(mx)
        work = jnp.where(pick, -jnp.inf, work)

    exps = [jnp.exp(v - vals[0]) for v in vals]
    denom = exps[0]
    for e in exps[1:]:
        denom = denom + e

    chosen = jnp.zeros((tm, LANES), F32)
    for pick in picks:
        chosen = chosen + jnp.where(pick, 1.0, 0.0)
    row = lax.broadcasted_iota(jnp.int32, (tm, tm), 0)
    col = lax.broadcasted_iota(jnp.int32, (tm, tm), 1)
    strict_lower = jnp.where(col < row, 1.0, 0.0).astype(BF16)
    before = jnp.dot(strict_lower, chosen.astype(BF16), preferred_element_type=F32) + carry_ref[...]

    e_out = jnp.zeros((tm, LANES), jnp.int32)
    gate_out = jnp.zeros((tm, LANES), F32)
    rank_out = jnp.zeros((tm, LANES), F32)
    for k in range(TOP_K):
        idx = jnp.min(jnp.where(picks[k], lane, LANES), axis=-1, keepdims=True)
        rank = jnp.sum(jnp.where(picks[k], before, 0.0), axis=-1, keepdims=True)
        e_out = jnp.where(lane == k, idx, e_out)
        gate_out = jnp.where(lane == k, exps[k] / denom, gate_out)
        rank_out = jnp.where(lane == k, rank, rank_out)
    e_ref[...] = e_out
    gate_ref[...] = gate_out
    rank_ref[...] = rank_out.astype(jnp.int32)

    carry_ref[...] = carry_ref[...] + jnp.sum(chosen, axis=0, keepdims=True)
    cnt_ref[...] = carry_ref[...]


def _ln_router(x, mix, g, b, rw_hi, rw_lo, rb, alpha):
    n, d = x.shape
    tm = _pick_tile(n, (ROW_TILE, 128))
    row = lambda w: pl.BlockSpec((tm, w), lambda i: (i, 0))
    const = lambda r, w: pl.BlockSpec((r, w), lambda i: (0, 0))
    return pl.pallas_call(
        functools.partial(_ln_router_kernel, alpha=alpha),
        out_shape=(jax.ShapeDtypeStruct((n, d), F32),
                   jax.ShapeDtypeStruct((n, d // (2 * LANES), LANES), jnp.uint32),
                   jax.ShapeDtypeStruct((n, LANES), jnp.int32),
                   jax.ShapeDtypeStruct((n, LANES), F32),
                   jax.ShapeDtypeStruct((n, LANES), jnp.int32),
                   jax.ShapeDtypeStruct((1, LANES), F32)),
        grid=(n // tm,),
        in_specs=[row(d), row(d), const(1, d), const(1, d), const(d, LANES), const(d, LANES),
                  const(1, LANES)],
        out_specs=(row(d), pl.BlockSpec((tm, d // (2 * LANES), LANES), lambda i: (i, 0, 0)),
                   row(LANES), row(LANES), row(LANES), const(1, LANES)),
        scratch_shapes=[pltpu.VMEM((1, LANES), F32)],
        compiler_params=_params(("arbitrary",)),
        name="ln_router",
    )(x, mix, g, b, rw_hi, rw_lo, rb)


def _expert_kernel(blk_e_ref, n_used_ref, row_tok_ref,
                   x_hbm, wgu_ref, bgu_ref, wdn_ref, bdn_ref, y_ref, xbuf, sem, *, bm, d_expert):
    i = pl.program_id(0)
    n_used = n_used_ref[0]

    def gather_copy(tok, r, slot):
        return pltpu.make_async_copy(x_hbm.at[tok], xbuf.at[slot, :, r, :], sem.at[slot])

    def issue(blk, slot):
        def body(r, _):
            gather_copy(row_tok_ref[blk * bm + r], r, slot).start()
            return 0
        lax.fori_loop(0, bm, body, 0, unroll=8)

    @pl.when(jnp.logical_and(i == 0, n_used > 0))
    def _():
        issue(0, 0)

    @pl.when(i + 1 < n_used)
    def _():
        issue(i + 1, (i + 1) % 2)

    @pl.when(i < n_used)
    def _():
        slot = i % 2

        def wait_body(r, _):
            gather_copy(0, r, slot).wait()
            return 0
        lax.fori_loop(0, bm, wait_body, 0, unroll=8)

        xb = _unpack_rows([xbuf[slot, c] for c in range(xbuf.shape[1])])
        hb = jnp.dot(xb, wgu_ref[...], preferred_element_type=F32) + bgu_ref[...]
        g = jnp.minimum(hb[:, :d_expert], SWIGLU_LIMIT)
        u = jnp.clip(hb[:, d_expert:], -SWIGLU_LIMIT, SWIGLU_LIMIT)
        act = (u + 1.0) * (g * (1.0 / (1.0 + jnp.exp(-SWIGLU_ALPHA * g))))
        y_ref[...] = jnp.dot(act.astype(BF16), wdn_ref[...], preferred_element_type=F32) + bdn_ref[...]

    @pl.when(i >= n_used)
    def _():
        y_ref[...] = jnp.zeros_like(y_ref)


def _experts(xp, blk_e, n_used, row_tok, w_gu, b_gu, w_dn, b_dn, layer, bm):
    _, _, d, f2 = w_gu.shape
    d_expert = f2 // 2
    n_rows = row_tok.shape[0]
    n_blk = n_rows // bm
    wmap = lambda i, be, nu, rt: (layer, be[i], 0, 0)
    grid_spec = pltpu.PrefetchScalarGridSpec(
        num_scalar_prefetch=3,
        grid=(n_blk,),
        in_specs=[pl.BlockSpec(memory_space=pl.ANY),
                  pl.BlockSpec((None, None, d, f2), wmap),
                  pl.BlockSpec((None, None, 1, f2), wmap),
                  pl.BlockSpec((None, None, d_expert, d), wmap),
                  pl.BlockSpec((None, None, 1, d), wmap)],
        out_specs=pl.BlockSpec((bm, d), lambda i, be, nu, rt: (i, 0)),
        scratch_shapes=[pltpu.VMEM((2, d // (2 * LANES), bm, LANES), jnp.uint32), pltpu.SemaphoreType.DMA((2,))],
    )
    return pl.pallas_call(
        functools.partial(_expert_kernel, bm=bm, d_expert=d_expert),
        out_shape=jax.ShapeDtypeStruct((n_rows, d), F32),
        grid_spec=grid_spec,
        compiler_params=_params(("arbitrary",)),
        name="experts",
    )(blk_e, n_used, row_tok, xp, w_gu, b_gu[:, :, None, :], w_dn, b_dn[:, :, None, :])


def _combine_kernel(pos_ref, y_hbm, x1_ref, gate_ref, g_ref, b_ref, x2_ref, xb_ref, ybuf, sem, *, tm, alpha):
    i = pl.program_id(0)
    n_steps = pl.num_programs(0)

    def gather_copy(src_row, k, t, slot):
        return pltpu.make_async_copy(y_hbm.at[pl.ds(src_row, 1)], ybuf.at[slot, k, pl.ds(t, 1)], sem.at[slot])

    def issue(step, slot):
        def body(t, _):
            for k in range(TOP_K):
                gather_copy(pos_ref[(step * tm + t) * TOP_K + k], k, t, slot).start()
            return 0
        lax.fori_loop(0, tm, body, 0, unroll=2)

    @pl.when(i == 0)
    def _():
        issue(0, 0)

    @pl.when(i + 1 < n_steps)
    def _():
        issue(i + 1, (i + 1) % 2)

    slot = i % 2

    def wait_body(t, _):
        for k in range(TOP_K):
            gather_copy(0, k, t, slot).wait()
        return 0
    lax.fori_loop(0, tm, wait_body, 0, unroll=2)

    gate = gate_ref[...]
    ffn = gate[:, 0:1] * ybuf[slot, 0]
    for k in range(1, TOP_K):
        ffn = ffn + gate[:, k:k + 1] * ybuf[slot, k]
    x2 = _layer_norm(alpha * x1_ref[...] + ffn, g_ref[...], b_ref[...])
    x2_ref[...] = x2
    xb_ref[...] = x2.astype(BF16)


def _combine_ln(pos_flat, y, x1, gate, g, b, alpha):
    n, d = x1.shape
    tm = _pick_tile(n, (COMBINE_TM,))
    grid_spec = pltpu.PrefetchScalarGridSpec(
        num_scalar_prefetch=1,
        grid=(n // tm,),
        in_specs=[pl.BlockSpec(memory_space=pl.ANY),
                  pl.BlockSpec((tm, d), lambda i, p: (i, 0)),
                  pl.BlockSpec((tm, LANES), lambda i, p: (i, 0)),
                  pl.BlockSpec((1, d), lambda i, p: (0, 0)),
                  pl.BlockSpec((1, d), lambda i, p: (0, 0))],
        out_specs=(pl.BlockSpec((tm, d), lambda i, p: (i, 0)),
                   pl.BlockSpec((tm, d), lambda i, p: (i, 0))),
        scratch_shapes=[pltpu.VMEM((2, TOP_K, tm, d), F32), pltpu.SemaphoreType.DMA((2,))],
    )
    return pl.pallas_call(
        functools.partial(_combine_kernel, tm=tm, alpha=alpha),
        out_shape=(jax.ShapeDtypeStruct((n, d), F32), jax.ShapeDtypeStruct((n, d), BF16)),
        grid_spec=grid_spec,
        compiler_params=_params(("arbitrary",)),
        name="combine_ln",
    )(pos_flat, y, x1, gate, g, b)


def _pad_cols(w, width):
    return jnp.pad(w, ((0, 0), (0, width - w.shape[1])))


def _moe_layer(x, mix, ln_g, ln_b, router_w, router_b, w_gu, b_gu, w_dn, b_dn, layer, ln2_g, ln2_b, alpha):
    n, d = x.shape
    n_exp = router_w.shape[1]
    bm = EXPERT_BM

    rw = _pad_cols(router_w, LANES)
    rw_hi = rw.astype(BF16)
    rw_lo = (rw - rw_hi.astype(F32)).astype(BF16)
    rb = jnp.full((1, LANES), MASK_VALUE, F32).at[0, :n_exp].set(router_b)
    x1, xp, top_e, gate, rank, cnt = _ln_router(x, mix, ln_g[None], ln_b[None], rw_hi, rw_lo, rb, alpha)

    counts = cnt[0, :n_exp].astype(jnp.int32)
    padded = (counts + bm - 1) // bm * bm
    pend = jnp.cumsum(padded)
    pstart = pend - padded
    top_e = top_e[:, :TOP_K]
    pos = (pstart[top_e] + rank[:, :TOP_K]).reshape(-1)
    n_blk = -(-(n * TOP_K + n_exp * (bm - 1)) // bm)
    n_rows = n_blk * bm
    tok = jnp.repeat(jnp.arange(n, dtype=jnp.int32), TOP_K)
    row_tok = jnp.zeros((n_rows,), jnp.int32).at[pos].set(tok)
    blk_start = jnp.arange(n_blk, dtype=jnp.int32) * bm
    blk_e = jnp.minimum(jnp.sum((pend[None, :] <= blk_start[:, None]).astype(jnp.int32), axis=1), n_exp - 1)
    n_used = (pend[-1:] // bm).astype(jnp.int32)

    y = _experts(xp, blk_e, n_used, row_tok, w_gu, b_gu, w_dn, b_dn, layer, bm)
    return _combine_ln(pos, y, x1, gate, ln2_g[None], ln2_b[None], alpha)


def _even_mixer(xb, w_in, b_f, w_o, batch, seq):
    sbw = SB_HEADS * HEAD_DIM
    fxw = FOX_HEADS * HEAD_DIM
    main = 3 * sbw + 3 * fxw
    zscale = HEAD_DIM ** -0.5 * LOG2E
    col = jnp.arange(main)
    is_q = (col < sbw) | ((col >= 3 * sbw) & (col < 3 * sbw + fxw))
    h = _matmul(xb, (w_in[:, :main] * jnp.where(is_q, zscale, 1.0)).astype(BF16), BF16)
    w_f = _pad_cols(w_in[:, main:], LANES).astype(BF16)
    cum = _fox_gate(xb, w_f, _pad_cols(b_f[None], LANES), batch, seq)
    o_sb = _sb_attention(h, batch, seq, SB_HEADS, 0, SB_HEADS, 2 * SB_HEADS)
    c0 = 3 * SB_HEADS
    o_fx = _fox_attention(h, cum, batch, seq, FOX_HEADS, c0, c0 + FOX_HEADS, c0 + 2 * FOX_HEADS)
    o = jnp.concatenate([o_sb, o_fx], axis=-1)
    return _matmul(o, w_o.astype(BF16), F32)


def _mla_mixer(xb, pos, w_down, q_norm, kv_norm, w_uq, w_ukv, w_o, batch, seq):
    hds = MLA_HEADS
    w_down_p = _pad_cols(w_down, Q_LORA + KV_LORA + LANES).astype(BF16)
    down = _matmul(xb, w_down_p, F32)
    freq = ROPE_THETA ** (-jnp.arange(0, ROPE_DIM, 2, dtype=F32) / ROPE_DIM)
    inv_freq = jnp.concatenate([freq, freq, jnp.zeros((LANES - ROPE_DIM,), F32)])[None]
    cq, ckv, k_pe, ra, rb, rc = _mla_prep(down, pos, q_norm[None], kv_norm[None], inv_freq)

    wq = w_uq.reshape(Q_LORA, hds, NOPE_DIM + ROPE_DIM)
    wq = jnp.pad(wq, ((0, 0), (0, 0), (0, LANES - ROPE_DIM))).reshape(Q_LORA, hds * (NOPE_DIM + LANES))
    wkv = w_ukv.reshape(KV_LORA, hds, NOPE_DIM + V_DIM)
    wkv = jnp.concatenate([wkv[:, :, :NOPE_DIM].reshape(KV_LORA, hds * NOPE_DIM),
                           wkv[:, :, NOPE_DIM:].reshape(KV_LORA, hds * V_DIM)], axis=1)
    q = _matmul(cq, (wq * ((NOPE_DIM + ROPE_DIM) ** -0.5 * LOG2E)).astype(BF16), BF16)
    kv = _matmul(ckv, wkv.astype(BF16), BF16)
    o = _mla_attention(q, kv, k_pe, ra, rb, rc, batch, seq, hds)
    return _matmul(o, w_o.astype(BF16), F32)


def kernel(x, positions, ln_mix_g, ln_mix_b, ln_ffn_g, ln_ffn_b, even_w_in, fox_b_f, even_w_o, mla_w_down, mla_q_norm, mla_kv_norm, mla_w_uq, mla_w_ukv, mla_w_o, router_w, router_b, expert_w_gate_up, expert_b_gate_up, expert_w_down, expert_b_down):
    batch, seq, d = x.shape
    depth = ln_mix_g.shape[0]
    alpha = (2 * depth) ** 0.25
    n = batch * seq
    xf = x.reshape(n, d)
    xb = xf.astype(BF16)
    pos = positions.reshape(n, 1)
    w_gu = expert_w_gate_up.astype(BF16)
    w_dn = expert_w_down.astype(BF16)
    for layer in range(depth):
        i = layer // 2
        if layer % 2 == 0:
            mix = _even_mixer(xb, even_w_in[i], fox_b_f[i], even_w_o[i], batch, seq)
        else:
            mix = _mla_mixer(xb, pos, mla_w_down[i], mla_q_norm[i], mla_kv_norm[i],
                             mla_w_uq[i], mla_w_ukv[i], mla_w_o[i], batch, seq)
        xf, xb = _moe_layer(xf, mix, ln_mix_g[layer], ln_mix_b[layer], router_w[layer], router_b[layer],
                            w_gu, expert_b_gate_up, w_dn, expert_b_down, layer,
                            ln_ffn_g[layer], ln_ffn_b[layer], alpha)
    return xf.reshape(batch, seq, d)
```

```python
import functools

import jax
import jax.numpy as jnp
from jax import lax
from jax.experimental import pallas as pl
from jax.experimental.pallas import tpu as pltpu

F32 = jnp.float32
BF16 = jnp.bfloat16

HEAD_DIM = 128
SB_HEADS = 8
FOX_HEADS = 8
MLA_HEADS = 16
Q_LORA = 512
KV_LORA = 512
NOPE_DIM = 128
ROPE_DIM = 64
V_DIM = 128
ROPE_THETA = 10000.0
N_EXPERTS = 32
TOP_K = 4
SWIGLU_LIMIT = 7.0
SWIGLU_ALPHA = 1.702
LN_EPS = 1e-5
RMS_EPS = 1e-6

LANES = 128
MASK_VALUE = -1e30
VMEM_LIMIT = 56 * 1024 * 1024
LOG2E = 1.4426950408889634
LN2 = 0.6931471805599453

MM_TM = 1024
MM_TN = 512
SB_TQ = 512
SB_CK = 256
SB_GROUP = 4
FA_TILE = 512
FA_GROUP = 4
ROW_TILE = 256
GATE_TILE = 512
EXPERT_BM = 512
GATHER_AHEAD = 2
COMBINE_TM = 128


def _params(sem):
    return pltpu.CompilerParams(dimension_semantics=sem, vmem_limit_bytes=VMEM_LIMIT)


def _pick_tile(n, prefs):
    for t in prefs:
        if n % t == 0:
            return t
    return n


def _mm_kernel(a_ref, w_ref, o_ref):
    o_ref[...] = jnp.dot(a_ref[...].astype(BF16), w_ref[...],
                         preferred_element_type=F32).astype(o_ref.dtype)


def _matmul(a, w, out_dtype):
    m, k = a.shape
    n = w.shape[1]
    tm = _pick_tile(m, (MM_TM, 512, 256, 128))
    tn = _pick_tile(n, (MM_TN, 384, 256, 128))
    return pl.pallas_call(
        _mm_kernel,
        out_shape=jax.ShapeDtypeStruct((m, n), out_dtype),
        grid=(m // tm, n // tn),
        in_specs=[pl.BlockSpec((tm, k), lambda i, j: (i, 0)),
                  pl.BlockSpec((k, tn), lambda i, j: (0, j))],
        out_specs=pl.BlockSpec((tm, tn), lambda i, j: (i, j)),
        compiler_params=_params(("parallel", "parallel")),
        name="matmul",
    )(a, w)


def _split2(x):
    hi = x.astype(BF16)
    lo = (x - hi.astype(F32)).astype(BF16)
    return hi, lo


def _split3(x):
    h1 = x.astype(BF16)
    r1 = x - h1.astype(F32)
    h2 = r1.astype(BF16)
    h3 = (r1 - h2.astype(F32)).astype(BF16)
    return h1, h2, h3


def _log1pexp_neg_abs(z):
    return jnp.log(1.0 + jnp.exp(-jnp.abs(z)))


def _neg_abs(x):
    bits = lax.bitcast_convert_type(x, jnp.uint32) | jnp.uint32(0x80000000)
    return lax.bitcast_convert_type(bits, F32)


def _store_packed_rows(xp_ref, x):
    half = x.shape[1] // 2
    lo = lax.bitcast_convert_type(x[:, :half].astype(BF16).astype(F32), jnp.uint32)
    hi = lax.bitcast_convert_type(x[:, half:].astype(BF16).astype(F32), jnp.uint32)
    w = (lo >> 16) | (hi & jnp.uint32(0xFFFF0000))
    for c in range(half // LANES):
        xp_ref[:, c, :] = w[:, c * LANES:(c + 1) * LANES]


def _unpack_rows(planes):
    los = [lax.bitcast_convert_type(w << 16, F32).astype(BF16) for w in planes]
    his = [lax.bitcast_convert_type(w & jnp.uint32(0xFFFF0000), F32).astype(BF16) for w in planes]
    return jnp.concatenate(los + his, axis=1)


def _layer_norm(h, g, b):
    mu = jnp.mean(h, axis=-1, keepdims=True)
    c = h - mu
    var = jnp.mean(c * c, axis=-1, keepdims=True)
    return c * lax.rsqrt(var + LN_EPS) * g + b


def _head_cols(hd):
    return slice(hd * HEAD_DIM, (hd + 1) * HEAD_DIM)


def _fox_gate_kernel(x_ref, w_ref, b_ref, o_ref, carry_ref):
    @pl.when(pl.program_id(1) == 0)
    def _():
        carry_ref[...] = jnp.zeros_like(carry_ref)

    f = jnp.dot(x_ref[...], w_ref[...], preferred_element_type=F32) + b_ref[...]
    log_f = jnp.minimum(f, 0.0) - _log1pexp_neg_abs(f)
    ts = log_f.shape[0]
    row = lax.broadcasted_iota(jnp.int32, (ts, ts), 0)
    col = lax.broadcasted_iota(jnp.int32, (ts, ts), 1)
    tri = jnp.where(col <= row, 1.0, 0.0).astype(BF16)
    cum = carry_ref[...]
    for part in _split3(log_f):
        cum = cum + jnp.dot(tri, part, preferred_element_type=F32)
    o_ref[...] = cum
    carry_ref[...] = cum[ts - 1:ts, :]


def _fox_gate(xb, w_f, b_f, batch, seq):
    n, d = xb.shape
    ts = _pick_tile(seq, (GATE_TILE, 256, 128))
    ns = seq // ts
    return pl.pallas_call(
        _fox_gate_kernel,
        out_shape=jax.ShapeDtypeStruct((n, LANES), F32),
        grid=(batch, ns),
        in_specs=[pl.BlockSpec((ts, d), lambda b, j: (b * ns + j, 0)),
                  pl.BlockSpec((d, LANES), lambda b, j: (0, 0)),
                  pl.BlockSpec((1, LANES), lambda b, j: (0, 0))],
        out_specs=pl.BlockSpec((ts, LANES), lambda b, j: (b * ns + j, 0)),
        scratch_shapes=[pltpu.VMEM((1, LANES), F32)],
        compiler_params=_params(("parallel", "arbitrary")),
        name="fox_gate",
    )(xb, w_f, b_f)


def _sb_kernel(q_ref, k_ref, v_ref, u_ref, o_ref, *, tq, ck, group):
    qi = pl.program_id(2)
    n_diag = tq // ck
    n_below = qi * n_diag

    def evaluate(units):
        n = len(units)
        starts = [pl.multiple_of(kb * ck, ck) for _, kb, _, _ in units]
        zs, sps, sums, stricts, laters = [None] * n, [None] * n, [None] * n, [None] * n, [None] * n
        pvs, tots = [None] * n, [None] * n

        def scores(u):
            hd, _, r0, masked = units[u]
            z = lax.dot_general(q_ref[r0:, _head_cols(hd)], k_ref[pl.ds(starts[u], ck), _head_cols(hd)],
                                (((1,), (1,)), ((), ())), preferred_element_type=F32)
            sp = jnp.maximum(z, 0.0) + jnp.log(1.0 + jnp.exp2(_neg_abs(z))) * (1.0 / LN2)
            zs[u], sps[u], sums[u] = z, sp, sp
            if masked:
                q_pos = qi * tq + r0 + lax.broadcasted_iota(jnp.int32, z.shape, 0)
                stricts[u] = (starts[u] + lax.broadcasted_iota(jnp.int32, z.shape, 1)) < q_pos
                sums[u] = jnp.where(stricts[u], sp, 0.0)

        def prefix(u):
            laters[u] = jnp.dot(sums[u].astype(BF16), u_ref[...], preferred_element_type=F32)

        def values(u):
            hd = units[u][0]
            w = jnp.exp2(zs[u] - sps[u] - laters[u])
            if stricts[u] is not None:
                w = jnp.where(stricts[u], w, 0.0)
            pvs[u] = jnp.dot(w.astype(BF16), v_ref[pl.ds(starts[u], ck), _head_cols(hd)],
                             preferred_element_type=F32)
            tots[u] = laters[u][:, :1] + sums[u][:, :1]

        for t in range(n + 2):
            if t >= 2:
                values(t - 2)
            if 1 <= t <= n:
                prefix(t - 1)
            if t < n:
                scores(t)
        return pvs, tots

    def fold(acc, later, pv, tot, r0):
        acc_s = acc[r0:] + pv * jnp.exp2(-later[r0:])
        later_s = later[r0:] + tot
        if r0:
            return (jnp.concatenate([acc[:r0], acc_s], axis=0),
                    jnp.concatenate([later[:r0], later_s], axis=0))
        return acc_s, later_s

    diag_units = [(hd, n_below + d, d * ck, True) for hd in range(group) for d in range(n_diag - 1, -1, -1)]
    pvs, tots = evaluate(diag_units)
    accs, laters = [], []
    for hd in range(group):
        acc, later = jnp.zeros((tq, HEAD_DIM), F32), jnp.zeros((tq, 1), F32)
        for u in range(hd * n_diag, (hd + 1) * n_diag):
            acc, later = fold(acc, later, pvs[u], tots[u], diag_units[u][2])
        accs.append(acc)
        laters.append(later)

    def body(i, carry):
        accs, laters = carry
        kb = n_below - 1 - 2 * i
        pvs, tots = evaluate([(hd, kb - c, 0, False) for hd in range(group) for c in range(2)])
        out = [fold(*fold(accs[hd], laters[hd], pvs[2 * hd], tots[2 * hd], 0),
                    pvs[2 * hd + 1], tots[2 * hd + 1], 0) for hd in range(group)]
        return tuple(o[0] for o in out), tuple(o[1] for o in out)

    accs, _ = lax.fori_loop(0, n_below // 2, body, (tuple(accs), tuple(laters)))
    for hd in range(group):
        o_ref[:, _head_cols(hd)] = accs[hd].astype(o_ref.dtype)


def _sb_attention(h, batch, seq, heads, q_col, k_col, v_col):
    n = h.shape[0]
    tq, ck = min(SB_TQ, seq), min(SB_CK, seq // 2)
    nq = seq // tq
    group = SB_GROUP
    assert (tq // ck) % 2 == 0 and heads % group == 0
    assert q_col % group == 0 and k_col % group == 0 and v_col % group == 0
    gw = group * HEAD_DIM
    j = lax.broadcasted_iota(jnp.int32, (ck, ck), 0)
    s = lax.broadcasted_iota(jnp.int32, (ck, ck), 1)
    u = jnp.where(j > s, 1.0, 0.0).astype(BF16)
    kern = functools.partial(_sb_kernel, tq=tq, ck=ck, group=group)
    return pl.pallas_call(
        kern,
        out_shape=jax.ShapeDtypeStruct((n, heads * HEAD_DIM), BF16),
        grid=(batch, heads // group, nq),
        in_specs=[pl.BlockSpec((tq, gw), lambda b, g, i: (b * nq + i, q_col // group + g)),
                  pl.BlockSpec((seq, gw), lambda b, g, i: (b, k_col // group + g)),
                  pl.BlockSpec((seq, gw), lambda b, g, i: (b, v_col // group + g)),
                  pl.BlockSpec((ck, ck), lambda b, g, i: (0, 0))],
        out_specs=pl.BlockSpec((tq, gw), lambda b, g, i: (b * nq + i, g)),
        compiler_params=_params(("parallel", "parallel", "arbitrary")),
        name="sb_attention",
    )(h, h, h, u)


def _flash_sweep(qs, k_fn, v_fn, bias_fn, qi, tile, dv):
    n = len(qs)

    def sweep_step(kb, states, masked):
        start = pl.multiple_of(kb * tile, tile)
        scores, new = [None] * n, [None] * n

        def score(hd):
            s = lax.dot_general(qs[hd], k_fn(hd, start), (((1,), (1,)), ((), ())), preferred_element_type=F32)
            if bias_fn is not None:
                s = s + bias_fn(hd, kb)
            if masked:
                s = jnp.where(lax.broadcasted_iota(jnp.int32, s.shape, 1)
                              <= lax.broadcasted_iota(jnp.int32, s.shape, 0), s, MASK_VALUE)
            scores[hd] = s

        def update(hd):
            m, l, acc = states[hd]
            s = scores[hd]
            m_new = jnp.maximum(m, jnp.max(s, axis=-1, keepdims=True))
            p = jnp.exp2(s - m_new)
            corr = jnp.exp2(m - m_new)
            l = corr * l + jnp.sum(p, axis=-1, keepdims=True)
            acc = corr * acc + jnp.dot(p.astype(BF16), v_fn(hd, start), preferred_element_type=F32)
            new[hd] = (m_new, l, acc)

        for i in range(n + 1):
            if i < n:
                score(i)
            if i >= 1:
                update(i - 1)
        return tuple(new)

    init = (jnp.full((tile, 1), MASK_VALUE, F32), jnp.zeros((tile, 1), F32), jnp.zeros((tile, dv), F32))
    states = lax.fori_loop(0, qi, lambda kb, st: sweep_step(kb, st, False), tuple(init for _ in range(n)))
    states = sweep_step(qi, states, True)
    return [acc / l for _, l, acc in states]


def _fox_kernel(q_ref, k_ref, v_ref, ck_ref, o_ref, *, tile, group):
    qs = [q_ref[:, _head_cols(hd)] for hd in range(group)]
    outs = _flash_sweep(
        qs,
        lambda hd, start: k_ref[pl.ds(start, tile), _head_cols(hd)],
        lambda hd, start: v_ref[pl.ds(start, tile), _head_cols(hd)],
        lambda hd, kb: ck_ref[hd, pl.ds(kb, 1), :] * (-LOG2E),
        pl.program_id(2), tile, HEAD_DIM)
    for hd in range(group):
        o_ref[:, _head_cols(hd)] = outs[hd].astype(o_ref.dtype)


def _fox_attention(h, cum, batch, seq, heads, q_col, k_col, v_col):
    n = h.shape[0]
    tile = min(FA_TILE, seq)
    nq = seq // tile
    group = FA_GROUP
    assert heads % group == 0 and q_col % group == 0 and k_col % group == 0 and v_col % group == 0
    gw = group * HEAD_DIM
    cum_k = jnp.transpose(cum[:, :heads].reshape(batch, seq, heads), (0, 2, 1)).reshape(batch, heads, nq, tile)
    kern = functools.partial(_fox_kernel, tile=tile, group=group)
    return pl.pallas_call(
        kern,
        out_shape=jax.ShapeDtypeStruct((n, heads * HEAD_DIM), BF16),
        grid=(batch, heads // group, nq),
        in_specs=[pl.BlockSpec((tile, gw), lambda b, g, i: (b * nq + i, q_col // group + g)),
                  pl.BlockSpec((seq, gw), lambda b, g, i: (b, k_col // group + g)),
                  pl.BlockSpec((seq, gw), lambda b, g, i: (b, v_col // group + g)),
                  pl.BlockSpec((None, group, nq, tile), lambda b, g, i: (b, g, 0, 0))],
        out_specs=pl.BlockSpec((tile, gw), lambda b, g, i: (b * nq + i, g)),
        compiler_params=_params(("parallel", "parallel", "arbitrary")),
        name="fox_attention",
    )(h, h, h, cum_k)


def _rope_lanes(t, ra, rb, rc):
    half = ROPE_DIM // 2
    return t * ra + pltpu.roll(t, half, 1) * rb + pltpu.roll(t, LANES - half, 1) * rc


def _mla_kernel(q_ref, kn_ref, kp_ref, v_ref, ra_ref, rb_ref, rc_ref, o_ref, kfull_ref,
                *, tile, group):
    dq = NOPE_DIM + LANES

    @pl.when(pl.program_id(2) == 0)
    def _():
        for hd in range(group):
            kfull_ref[:, hd * dq:hd * dq + NOPE_DIM] = kn_ref[:, _head_cols(hd)]
            kfull_ref[:, hd * dq + NOPE_DIM:(hd + 1) * dq] = kp_ref[...]

    qs = []
    for hd in range(group):
        q = q_ref[:, hd * dq:(hd + 1) * dq]
        q_pe = _rope_lanes(q[:, NOPE_DIM:].astype(F32), ra_ref[...], rb_ref[...], rc_ref[...])
        qs.append(jnp.concatenate([q[:, :NOPE_DIM], q_pe.astype(BF16)], axis=1))
    outs = _flash_sweep(
        qs,
        lambda hd, start: kfull_ref[pl.ds(start, tile), hd * dq:(hd + 1) * dq],
        lambda hd, start: v_ref[pl.ds(start, tile), _head_cols(hd)],
        None, pl.program_id(2), tile, V_DIM)
    for hd in range(group):
        o_ref[:, _head_cols(hd)] = outs[hd].astype(o_ref.dtype)


def _mla_attention(q, kv, k_pe, ra, rb, rc, batch, seq, heads):
    n = q.shape[0]
    tile = min(FA_TILE, seq)
    nq = seq // tile
    group = FA_GROUP
    assert heads % group == 0
    ng = heads // group
    dq = NOPE_DIM + LANES
    kern = functools.partial(_mla_kernel, tile=tile, group=group)
    rope_spec = pl.BlockSpec((tile, LANES), lambda b, g, i: (b * nq + i, 0))
    return pl.pallas_call(
        kern,
        out_shape=jax.ShapeDtypeStruct((n, heads * V_DIM), BF16),
        grid=(batch, ng, nq),
        in_specs=[pl.BlockSpec((tile, group * dq), lambda b, g, i: (b * nq + i, g)),
                  pl.BlockSpec((seq, group * NOPE_DIM), lambda b, g, i: (b, g)),
                  pl.BlockSpec((seq, LANES), lambda b, g, i: (b, 0)),
                  pl.BlockSpec((seq, group * V_DIM), lambda b, g, i: (b, ng + g)),
                  rope_spec, rope_spec, rope_spec],
        out_specs=pl.BlockSpec((tile, group * V_DIM), lambda b, g, i: (b * nq + i, g)),
        scratch_shapes=[pltpu.VMEM((seq, group * dq), BF16)],
        compiler_params=_params(("parallel", "parallel", "arbitrary")),
        name="mla_attention",
    )(q, kv, k_pe, kv, ra, rb, rc)


def _mla_prep_kernel(down_ref, pos_ref, qn_ref, kvn_ref, invf_ref,
                     cq_ref, ckv_ref, kpe_ref, ra_ref, rb_ref, rc_ref):
    down = down_ref[...]

    def rms(c, g):
        return c * lax.rsqrt(jnp.mean(c * c, axis=-1, keepdims=True) + RMS_EPS) * g

    cq_ref[...] = rms(down[:, :Q_LORA], qn_ref[...]).astype(cq_ref.dtype)
    ckv_ref[...] = rms(down[:, Q_LORA:Q_LORA + KV_LORA], kvn_ref[...]).astype(ckv_ref.dtype)

    ang = pos_ref[...].astype(F32) * invf_ref[...]
    cos, sin = jnp.cos(ang), jnp.sin(ang)
    lane = lax.broadcasted_iota(jnp.int32, ang.shape, 1)
    half = ROPE_DIM // 2
    ra = jnp.where(lane < ROPE_DIM, cos, 0.0)
    rb = jnp.where((lane >= half) & (lane < ROPE_DIM), sin, 0.0)
    rc = jnp.where(lane < half, -sin, 0.0)
    ra_ref[...] = ra
    rb_ref[...] = rb
    rc_ref[...] = rc
    kpe_ref[...] = _rope_lanes(down[:, Q_LORA + KV_LORA:], ra, rb, rc).astype(kpe_ref.dtype)


def _mla_prep(down, pos, q_norm, kv_norm, inv_freq):
    n, dd = down.shape
    tm = _pick_tile(n, (ROW_TILE, 128))
    row = lambda w: pl.BlockSpec((tm, w), lambda i: (i, 0))
    const = lambda w: pl.BlockSpec((1, w), lambda i: (0, 0))
    return pl.pallas_call(
        _mla_prep_kernel,
        out_shape=(jax.ShapeDtypeStruct((n, Q_LORA), BF16),
                   jax.ShapeDtypeStruct((n, KV_LORA), BF16),
                   jax.ShapeDtypeStruct((n, LANES), BF16),
                   jax.ShapeDtypeStruct((n, LANES), F32),
                   jax.ShapeDtypeStruct((n, LANES), F32),
                   jax.ShapeDtypeStruct((n, LANES), F32)),
        grid=(n // tm,),
        in_specs=[row(dd), row(1), const(Q_LORA), const(KV_LORA), const(LANES)],
        out_specs=(row(Q_LORA), row(KV_LORA), row(LANES), row(LANES), row(LANES), row(LANES)),
        compiler_params=_params(("parallel",)),
        name="mla_prep",
    )(down, pos, q_norm, kv_norm, inv_freq)


def _ln_router_kernel(x_ref, mix_ref, g_ref, b_ref, rw_hi_ref, rw_lo_ref, rb_ref,
                      x1_ref, xp_ref, e_ref, gate_ref, rank_ref, cnt_ref, carry_ref, *, alpha):
    @pl.when(pl.program_id(0) == 0)
    def _():
        carry_ref[...] = jnp.zeros_like(carry_ref)

    x1 = _layer_norm(alpha * x_ref[...] + mix_ref[...], g_ref[...], b_ref[...])
    x1_ref[...] = x1
    _store_packed_rows(xp_ref, x1)

    a_hi, a_lo = _split2(x1)
    rw_hi, rw_lo = rw_hi_ref[...], rw_lo_ref[...]
    logits = (jnp.dot(a_hi, rw_hi, preferred_element_type=F32)
              + jnp.dot(a_hi, rw_lo, preferred_element_type=F32)
              + jnp.dot(a_lo, rw_hi, preferred_element_type=F32)) + rb_ref[...]

    tm = logits.shape[0]
    lane = lax.broadcasted_iota(jnp.int32, (tm, LANES), 1)
    work = logits
    picks, vals = [], []
    for _ in range(TOP_K):
        mx = jnp.max(work, axis=-1, keepdims=True)
        idx = jnp.min(jnp.where(work == mx, lane, LANES), axis=-1, keepdims=True)
        pick = lane == idx
        picks.append(pick)
        vals.append(mx)
        work = jnp.where(pick, -jnp.inf, work)

    exps = [jnp.exp(v - vals[0]) for v in vals]
    denom = exps[0]
    for e in exps[1:]:
        denom = denom + e

    chosen = jnp.zeros((tm, LANES), F32)
    for pick in picks:
        chosen = chosen + jnp.where(pick, 1.0, 0.0)
    row = lax.broadcasted_iota(jnp.int32, (tm, tm), 0)
    col = lax.broadcasted_iota(jnp.int32, (tm, tm), 1)
    strict_lower = jnp.where(col < row, 1.0, 0.0).astype(BF16)
    before = jnp.dot(strict_lower, chosen.astype(BF16), preferred_element_type=F32) + carry_ref[...]

    e_out = jnp.zeros((tm, LANES), jnp.int32)
    gate_out = jnp.zeros((tm, LANES), F32)
    rank_out = jnp.zeros((tm, LANES), F32)
    for k in range(TOP_K):
        idx = jnp.min(jnp.where(picks[k], lane, LANES), axis=-1, keepdims=True)
        rank = jnp.sum(jnp.where(picks[k], before, 0.0), axis=-1, keepdims=True)
        e_out = jnp.where(lane == k, idx, e_out)
        gate_out = jnp.where(lane == k, exps[k] / denom, gate_out)
        rank_out = jnp.where(lane == k, rank, rank_out)
    e_ref[...] = e_out
    gate_ref[...] = gate_out
    rank_ref[...] = rank_out.astype(jnp.int32)

    carry_ref[...] = carry_ref[...] + jnp.sum(chosen, axis=0, keepdims=True)
    cnt_ref[...] = carry_ref[...]


def _ln_router(x, mix, g, b, rw_hi, rw_lo, rb, alpha):
    n, d = x.shape
    tm = _pick_tile(n, (ROW_TILE, 128))
    row = lambda w: pl.BlockSpec((tm, w), lambda i: (i, 0))
    const = lambda r, w: pl.BlockSpec((r, w), lambda i: (0, 0))
    return pl.pallas_call(
        functools.partial(_ln_router_kernel, alpha=alpha),
        out_shape=(jax.ShapeDtypeStruct((n, d), F32),
                   jax.ShapeDtypeStruct((n, d // (2 * LANES), LANES), jnp.uint32),
                   jax.ShapeDtypeStruct((n, LANES), jnp.int32),
                   jax.ShapeDtypeStruct((n, LANES), F32),
                   jax.ShapeDtypeStruct((n, LANES), jnp.int32),
                   jax.ShapeDtypeStruct((1, LANES), F32)),
        grid=(n // tm,),
        in_specs=[row(d), row(d), const(1, d), const(1, d), const(d, LANES), const(d, LANES),
                  const(1, LANES)],
        out_specs=(row(d), pl.BlockSpec((tm, d // (2 * LANES), LANES), lambda i: (i, 0, 0)),
                   row(LANES), row(LANES), row(LANES), const(1, LANES)),
        scratch_shapes=[pltpu.VMEM((1, LANES), F32)],
        compiler_params=_params(("arbitrary",)),
        name="ln_router",
    )(x, mix, g, b, rw_hi, rw_lo, rb)


def _expert_kernel(blk_e_ref, n_used_ref, row_tok_ref,
                   x_hbm, wgu_ref, bgu_ref, wdn_ref, bdn_ref, y_ref, xbuf, sem, *, bm, d_expert):
    i = pl.program_id(0)
    n_used = n_used_ref[0]

    def gather_copy(tok, r, slot):
        return pltpu.make_async_copy(x_hbm.at[tok], xbuf.at[slot, :, r, :], sem.at[slot])

    def issue(blk, slot):
        def body(r, _):
            gather_copy(row_tok_ref[blk * bm + r], r, slot).start()
            return 0
        lax.fori_loop(0, bm, body, 0, unroll=8)

    for b in range(GATHER_AHEAD):
        @pl.when(jnp.logical_and(i == 0, n_used > b))
        def _(b=b):
            issue(b, b)

    @pl.when(i + GATHER_AHEAD < n_used)
    def _():
        issue(i + GATHER_AHEAD, (i + GATHER_AHEAD) % (GATHER_AHEAD + 1))

    @pl.when(i < n_used)
    def _():
        slot = i % (GATHER_AHEAD + 1)

        def wait_body(r, _):
            gather_copy(0, r, slot).wait()
            return 0
        lax.fori_loop(0, bm, wait_body, 0, unroll=8)

        xb = _unpack_rows([xbuf[slot, c] for c in range(xbuf.shape[1])])
        hb = jnp.dot(xb, wgu_ref[...], preferred_element_type=F32) + bgu_ref[...]
        g = jnp.minimum(hb[:, :d_expert], SWIGLU_LIMIT)
        u = jnp.clip(hb[:, d_expert:], -SWIGLU_LIMIT, SWIGLU_LIMIT)
        act = (u + 1.0) * (g * (1.0 / (1.0 + jnp.exp(-SWIGLU_ALPHA * g))))
        y_ref[...] = jnp.dot(act.astype(BF16), wdn_ref[...], preferred_element_type=F32) + bdn_ref[...]

    @pl.when(i >= n_used)
    def _():
        y_ref[...] = jnp.zeros_like(y_ref)


def _experts(xp, blk_e, n_used, row_tok, w_gu, b_gu, w_dn, b_dn, layer, bm):
    _, _, d, f2 = w_gu.shape
    d_expert = f2 // 2
    n_rows = row_tok.shape[0]
    n_blk = n_rows // bm
    wmap = lambda i, be, nu, rt: (layer, be[i], 0, 0)
    grid_spec = pltpu.PrefetchScalarGridSpec(
        num_scalar_prefetch=3,
        grid=(n_blk,),
        in_specs=[pl.BlockSpec(memory_space=pl.ANY),
                  pl.BlockSpec((None, None, d, f2), wmap),
                  pl.BlockSpec((None, None, 1, f2), wmap),
                  pl.BlockSpec((None, None, d_expert, d), wmap),
                  pl.BlockSpec((None, None, 1, d), wmap)],
        out_specs=pl.BlockSpec((bm, d), lambda i, be, nu, rt: (i, 0)),
        scratch_shapes=[pltpu.VMEM((GATHER_AHEAD + 1, d // (2 * LANES), bm, LANES), jnp.uint32),
                        pltpu.SemaphoreType.DMA((GATHER_AHEAD + 1,))],
    )
    return pl.pallas_call(
        functools.partial(_expert_kernel, bm=bm, d_expert=d_expert),
        out_shape=jax.ShapeDtypeStruct((n_rows, d), F32),
        grid_spec=grid_spec,
        compiler_params=_params(("arbitrary",)),
        name="experts",
    )(blk_e, n_used, row_tok, xp, w_gu, b_gu[:, :, None, :], w_dn, b_dn[:, :, None, :])


def _combine_kernel(pos_ref, y_hbm, x1_ref, gate_ref, g_ref, b_ref, x2_ref, xb_ref, ybuf, sem, *, tm, alpha):
    i = pl.program_id(0)
    n_steps = pl.num_programs(0)

    def gather_copy(src_row, k, t, slot):
        return pltpu.make_async_copy(y_hbm.at[pl.ds(src_row, 1)], ybuf.at[slot, k, pl.ds(t, 1)], sem.at[slot])

    def issue(step, slot):
        def body(t, _):
            for k in range(TOP_K):
                gather_copy(pos_ref[(step * tm + t) * TOP_K + k], k, t, slot).start()
            return 0
        lax.fori_loop(0, tm, body, 0, unroll=2)

    @pl.when(i == 0)
    def _():
        issue(0, 0)

    @pl.when(i + 1 < n_steps)
    def _():
        issue(i + 1, (i + 1) % 2)

    slot = i % 2

    def wait_body(t, _):
        for k in range(TOP_K):
            gather_copy(0, k, t, slot).wait()
        return 0
    lax.fori_loop(0, tm, wait_body, 0, unroll=2)

    gate = gate_ref[...]
    ffn = gate[:, 0:1] * ybuf[slot, 0]
    for k in range(1, TOP_K):
        ffn = ffn + gate[:, k:k + 1] * ybuf[slot, k]
    x2 = _layer_norm(alpha * x1_ref[...] + ffn, g_ref[...], b_ref[...])
    x2_ref[...] = x2
    xb_ref[...] = x2.astype(BF16)


def _combine_ln(pos_flat, y, x1, gate, g, b, alpha):
    n, d = x1.shape
    tm = _pick_tile(n, (COMBINE_TM,))
    grid_spec = pltpu.PrefetchScalarGridSpec(
        num_scalar_prefetch=1,
        grid=(n // tm,),
        in_specs=[pl.BlockSpec(memory_space=pl.ANY),
                  pl.BlockSpec((tm, d), lambda i, p: (i, 0)),
                  pl.BlockSpec((tm, LANES), lambda i, p: (i, 0)),
                  pl.BlockSpec((1, d), lambda i, p: (0, 0)),
                  pl.BlockSpec((1, d), lambda i, p: (0, 0))],
        out_specs=(pl.BlockSpec((tm, d), lambda i, p: (i, 0)),
                   pl.BlockSpec((tm, d), lambda i, p: (i, 0))),
        scratch_shapes=[pltpu.VMEM((2, TOP_K, tm, d), F32), pltpu.SemaphoreType.DMA((2,))],
    )
    return pl.pallas_call(
        functools.partial(_combine_kernel, tm=tm, alpha=alpha),
        out_shape=(jax.ShapeDtypeStruct((n, d), F32), jax.ShapeDtypeStruct((n, d), BF16)),
        grid_spec=grid_spec,
        compiler_params=_params(("arbitrary",)),
        name="combine_ln",
    )(pos_flat, y, x1, gate, g, b)


def _pad_cols(w, width):
    return jnp.pad(w, ((0, 0), (0, width - w.shape[1])))


def _moe_layer(x, mix, ln_g, ln_b, router_w, router_b, w_gu, b_gu, w_dn, b_dn, layer, ln2_g, ln2_b, alpha):
    n, d = x.shape
    n_exp = router_w.shape[1]
    bm = EXPERT_BM

    rw = _pad_cols(router_w, LANES)
    rw_hi = rw.astype(BF16)
    rw_lo = (rw - rw_hi.astype(F32)).astype(BF16)
    rb = jnp.full((1, LANES), MASK_VALUE, F32).at[0, :n_exp].set(router_b)
    x1, xp, top_e, gate, rank, cnt = _ln_router(x, mix, ln_g[None], ln_b[None], rw_hi, rw_lo, rb, alpha)

    counts = cnt[0, :n_exp].astype(jnp.int32)
    padded = (counts + bm - 1) // bm * bm
    pend = jnp.cumsum(padded)
    pstart = pend - padded
    top_e = top_e[:, :TOP_K]
    pos = (pstart[top_e] + rank[:, :TOP_K]).reshape(-1)
    n_blk = -(-(n * TOP_K + n_exp * (bm - 1)) // bm)
    n_rows = n_blk * bm
    tok = jnp.repeat(jnp.arange(n, dtype=jnp.int32), TOP_K)
    row_tok = jnp.zeros((n_rows,), jnp.int32).at[pos].set(tok)
    blk_start = jnp.arange(n_blk, dtype=jnp.int32) * bm
    blk_e = jnp.minimum(jnp.sum((pend[None, :] <= blk_start[:, None]).astype(jnp.int32), axis=1), n_exp - 1)
    n_used = (pend[-1:] // bm).astype(jnp.int32)

    y = _experts(xp, blk_e, n_used, row_tok, w_gu, b_gu, w_dn, b_dn, layer, bm)
    return _combine_ln(pos, y, x1, gate, ln2_g[None], ln2_b[None], alpha)


def _even_mixer(xb, w_in, b_f, w_o, batch, seq):
    sbw = SB_HEADS * HEAD_DIM
    fxw = FOX_HEADS * HEAD_DIM
    main = 3 * sbw + 3 * fxw
    zscale = HEAD_DIM ** -0.5 * LOG2E
    col = jnp.arange(main)
    is_q = (col < sbw) | ((col >= 3 * sbw) & (col < 3 * sbw + fxw))
    h = _matmul(xb, (w_in[:, :main] * jnp.where(is_q, zscale, 1.0)).astype(BF16), BF16)
    w_f = _pad_cols(w_in[:, main:], LANES).astype(BF16)
    cum = _fox_gate(xb, w_f, _pad_cols(b_f[None], LANES), batch, seq)
    o_sb = _sb_attention(h, batch, seq, SB_HEADS, 0, SB_HEADS, 2 * SB_HEADS)
    c0 = 3 * SB_HEADS
    o_fx = _fox_attention(h, cum, batch, seq, FOX_HEADS, c0, c0 + FOX_HEADS, c0 + 2 * FOX_HEADS)
    o = jnp.concatenate([o_sb, o_fx], axis=-1)
    return _matmul(o, w_o.astype(BF16), F32)


def _mla_mixer(xb, pos, w_down, q_norm, kv_norm, w_uq, w_ukv, w_o, batch, seq):
    hds = MLA_HEADS
    w_down_p = _pad_cols(w_down, Q_LORA + KV_LORA + LANES).astype(BF16)
    down = _matmul(xb, w_down_p, F32)
    freq = ROPE_THETA ** (-jnp.arange(0, ROPE_DIM, 2, dtype=F32) / ROPE_DIM)
    inv_freq = jnp.concatenate([freq, freq, jnp.zeros((LANES - ROPE_DIM,), F32)])[None]
    cq, ckv, k_pe, ra, rb, rc = _mla_prep(down, pos, q_norm[None], kv_norm[None], inv_freq)

    wq = w_uq.reshape(Q_LORA, hds, NOPE_DIM + ROPE_DIM)
    wq = jnp.pad(wq, ((0, 0), (0, 0), (0, LANES - ROPE_DIM))).reshape(Q_LORA, hds * (NOPE_DIM + LANES))
    wkv = w_ukv.reshape(KV_LORA, hds, NOPE_DIM + V_DIM)
    wkv = jnp.concatenate([wkv[:, :, :NOPE_DIM].reshape(KV_LORA, hds * NOPE_DIM),
                           wkv[:, :, NOPE_DIM:].reshape(KV_LORA, hds * V_DIM)], axis=1)
    q = _matmul(cq, (wq * ((NOPE_DIM + ROPE_DIM) ** -0.5 * LOG2E)).astype(BF16), BF16)
    kv = _matmul(ckv, wkv.astype(BF16), BF16)
    o = _mla_attention(q, kv, k_pe, ra, rb, rc, batch, seq, hds)
    return _matmul(o, w_o.astype(BF16), F32)


def kernel(x, positions, ln_mix_g, ln_mix_b, ln_ffn_g, ln_ffn_b, even_w_in, fox_b_f, even_w_o, mla_w_down, mla_q_norm, mla_kv_norm, mla_w_uq, mla_w_ukv, mla_w_o, router_w, router_b, expert_w_gate_up, expert_b_gate_up, expert_w_down, expert_b_down):
    batch, seq, d = x.shape
    depth = ln_mix_g.shape[0]
    alpha = (2 * depth) ** 0.25
    n = batch * seq
    xf = x.reshape(n, d)
    xb = xf.astype(BF16)
    pos = positions.reshape(n, 1)
    w_gu = expert_w_gate_up.astype(BF16)
    w_dn = expert_w_down.astype(BF16)
    for layer in range(depth):
        i = layer // 2
        if layer % 2 == 0:
            mix = _even_mixer(xb, even_w_in[i], fox_b_f[i], even_w_o[i], batch, seq)
        else:
            mix = _mla_mixer(xb, pos, mla_w_down[i], mla_q_norm[i], mla_kv_norm[i],
                             mla_w_uq[i], mla_w_ukv[i], mla_w_o[i], batch, seq)
        xf, xb = _moe_layer(xf, mix, ln_mix_g[layer], ln_mix_b[layer], router_w[layer], router_b[layer],
                            w_gu, expert_b_gate_up, w_dn, expert_b_down, layer,
                            ln_ffn_g[layer], ln_ffn_b[layer], alpha)
    return xf.reshape(batch, seq, d)
```
